```python
import math
import jax
import jax.numpy as jnp
from jax import lax
import numpy as np

D_MODEL = 1024
BATCH = 16
SEQ = 2048
DEPTH = 2
DEC_BATCH = 128
DEC_SEQ = 1
PAST_LEN = 8192
PAGE_SIZE = 128

HEAD_DIM = 64
MOBA_HEADS = 4
MOBA_BLOCK = 256
MOBA_TOPK = 3
MOBA_Q_BLOCK = 16
NSA_HEADS = 4
NSA_DK = 64
NSA_CMP_LEN = 32
NSA_CMP_STRIDE = 16
NSA_CMP_HIDDEN = 128
NSA_SLC_BLOCK = 64
NSA_TOPN = 16
NSA_WINDOW = 512
NSA_Q_BLOCK = 64
MLA_HEADS = 8
MLA_Q_RANK = 256
MLA_KV_RANK = 128
MLA_NOPE = 64
MLA_ROPE = 32
MLA_V = 64
MLA_Q_BLOCK = 128
ROPE_THETA = 10000.0
T5_BUCKETS = 32
T5_MAX_DIST = 128
D_FF = -(-8 * D_MODEL // 768) * 256
PLE_DIM = 256
EPS = 1e-6
MOBA_W = MOBA_HEADS * HEAD_DIM
NSA_W = NSA_HEADS * NSA_DK
MLA_W = MLA_HEADS * MLA_V
MIX_WIDTH = MOBA_W + NSA_W + MLA_W
IN_SIZES = (MOBA_W, MOBA_W, MOBA_W, NSA_W, 6 * NSA_DK, 3 * NSA_HEADS, MLA_Q_RANK, MLA_KV_RANK, MLA_ROPE, 3 * D_MODEL)
N_IN = sum(IN_SIZES)

kernel_name = 'hybrid_moba_nsa_mla_decode_step'


def rms_norm(x, g):
    x32 = x.astype(jnp.float32)
    y = x32 * lax.rsqrt(jnp.mean(x32 * x32, axis=-1, keepdims=True) + EPS)
    return (y * g.astype(jnp.float32)).astype(x.dtype)


def masked_softmax(logits, mask):
    logits = jnp.where(mask, logits.astype(jnp.float32), -jnp.inf)
    m = jnp.max(logits, axis=-1, keepdims=True)
    m = jnp.where(jnp.isfinite(m), m, 0.0)
    e = jnp.exp(logits - m)
    return e / jnp.maximum(jnp.sum(e, axis=-1, keepdims=True), 1e-30)


def t5_bucket(dist):
    n = jnp.maximum(dist, 0)
    max_exact = T5_BUCKETS // 2
    nf = jnp.maximum(n, 1).astype(jnp.float32)
    large = max_exact + (jnp.log(nf / max_exact) / math.log(T5_MAX_DIST / max_exact)
                         * (T5_BUCKETS - max_exact)).astype(jnp.int32)
    return jnp.where(n < max_exact, n, jnp.minimum(large, T5_BUCKETS - 1))


def rope(x, pos):
    half = MLA_ROPE // 2
    freq = ROPE_THETA ** (-jnp.arange(half, dtype=jnp.float32) / half)
    ang = pos.astype(jnp.float32)[:, None] * freq[None, :]
    shape = (1, pos.shape[0]) + (1,) * (x.ndim - 3) + (half,)
    cos = jnp.cos(ang).reshape(shape)
    sin = jnp.sin(ang).reshape(shape)
    x32 = x.astype(jnp.float32)
    x1, x2 = x32[..., :half], x32[..., half:]
    return jnp.concatenate([x1 * cos - x2 * sin, x1 * sin + x2 * cos], axis=-1).astype(x.dtype)


def locate(page_table, pos, n_new):
    n_pages = page_table.shape[1]
    past = n_pages * PAGE_SIZE
    bidx = jnp.arange(pos.shape[0]).reshape((-1,) + (1,) * (pos.ndim - 1))
    page = page_table[bidx, jnp.clip(pos // PAGE_SIZE, 0, n_pages - 1)]
    slot = pos % PAGE_SIZE
    j = jnp.clip(pos - past, 0, n_new - 1)
    return bidx, page, slot, j, pos < past


def moba_attend(q, t_pos, kmean, gather_kv, bias_tab):
    bsz, nq, nh, dh = q.shape
    nb = kmean.shape[1]
    qb = t_pos // MOBA_BLOCK
    q32 = q.astype(jnp.float32)
    gate = jnp.einsum('bqhd,bnhd->bqhn', q32, kmean.astype(jnp.float32))
    past_blk = jnp.arange(nb)[None, :] < qb[:, None]
    gate = jnp.where(past_blk[None, :, None, :], gate, -jnp.inf)
    n_sel = min(MOBA_TOPK, nb)
    _, sel = lax.top_k(gate, n_sel)
    sel_ok = sel < qb[None, :, None, None]
    offs = jnp.arange(MOBA_BLOCK)
    sel_pos = (sel[..., None] * MOBA_BLOCK + offs).reshape(bsz, nq, nh, n_sel * MOBA_BLOCK)
    own_pos = qb[:, None] * MOBA_BLOCK + offs
    own_shape = (bsz, nq, nh, MOBA_BLOCK)
    pos = jnp.concatenate([sel_pos, jnp.broadcast_to(own_pos[None, :, None, :], own_shape)], axis=-1)
    valid = jnp.concatenate([jnp.repeat(sel_ok, MOBA_BLOCK, axis=-1),
                             jnp.broadcast_to((own_pos <= t_pos[:, None])[None, :, None, :], own_shape)], axis=-1)
    k_g, v_g = gather_kv(pos)
    logits = jnp.einsum('bqhd,bqhnd->bqhn', q32, k_g.astype(jnp.float32)) / math.sqrt(dh)
    bias = bias_tab[t5_bucket(t_pos[None, :, None, None] - pos), jnp.arange(nh)[None, None, :, None]]
    p = masked_softmax(logits + bias.astype(jnp.float32), valid)
    return jnp.einsum('bqhn,bqhnd->bqhd', p, v_g.astype(jnp.float32)).astype(q.dtype)


def moba_prompt(q, k, v, bias_tab):
    bsz, s, nh, dh = q.shape
    nb = -(-s // MOBA_BLOCK)
    pad = ((0, 0), (0, nb * MOBA_BLOCK - s), (0, 0), (0, 0))
    k_pad, v_pad = jnp.pad(k, pad), jnp.pad(v, pad)
    kmean = jnp.mean(k_pad.astype(jnp.float32).reshape(bsz, nb, MOBA_BLOCK, nh, dh), axis=2)
    bidx = jnp.arange(bsz)[:, None, None, None]
    hidx = jnp.arange(nh)[None, None, :, None]

    def gather_kv(pos):
        return k_pad[bidx, pos, hidx], v_pad[bidx, pos, hidx]

    nqb = s // MOBA_Q_BLOCK
    qs = q.reshape(bsz, nqb, MOBA_Q_BLOCK, nh, dh).swapaxes(0, 1)
    ts = jnp.arange(s, dtype=jnp.int32).reshape(nqb, MOBA_Q_BLOCK)
    out = lax.map(lambda a: moba_attend(a[0], a[1], kmean, gather_kv, bias_tab), (qs, ts))
    return out.swapaxes(0, 1).reshape(bsz, s, nh, dh)


def moba_sample(q, k_new, v_new, cache, layer, page_table, bias_tab):
    bsz, s, nh, dh = q.shape
    n_pages = page_table.shape[1]
    past = n_pages * PAGE_SIZE
    t_pos = past + jnp.arange(s, dtype=jnp.int32)
    nb = -(-(past + s) // MOBA_BLOCK)
    page_sums = jnp.sum(cache[layer, :, :, 0], axis=1, dtype=jnp.float32)
    blk_sum = jnp.zeros((bsz, nb, nh, dh), jnp.float32)
    blk_sum = blk_sum.at[:, (jnp.arange(n_pages) * PAGE_SIZE) // MOBA_BLOCK].add(page_sums[page_table])
    blk_sum = blk_sum.at[:, t_pos // MOBA_BLOCK].add(k_new.astype(jnp.float32))
    kmean = blk_sum / MOBA_BLOCK
    hidx = jnp.arange(nh)[None, None, :, None]

    def gather_kv(pos):
        bidx, page, slot, j, is_past = locate(page_table, pos, s)
        m = is_past[..., None]
        return (jnp.where(m, cache[layer, page, slot, 0, hidx], k_new[bidx, j, hidx]),
                jnp.where(m, cache[layer, page, slot, 1, hidx], v_new[bidx, j, hidx]))

    return moba_attend(q, t_pos, kmean, gather_kv, bias_tab)


def nsa_subblocks(rows, w1):
    n, length, dk = rows.shape
    r = rows.astype(jnp.float32).reshape(n, length // NSA_CMP_STRIDE, NSA_CMP_STRIDE, dk)
    w = w1.astype(jnp.float32).reshape(NSA_CMP_LEN // NSA_CMP_STRIDE, NSA_CMP_STRIDE, dk, NSA_CMP_HIDDEN)
    return jnp.einsum('nmjd,hjdf->nmhf', r, w)


def nsa_compress(hsub, w1, b1, w2, b2, pos_emb):
    hpos = jnp.einsum('jd,jdf->f', pos_emb.astype(jnp.float32), w1.astype(jnp.float32))
    h = hsub[:, :-1, 0] + hsub[:, 1:, 1] + hpos + b1.astype(jnp.float32)
    return jax.nn.silu(h) @ w2.astype(jnp.float32) + b2.astype(jnp.float32)


def nsa_attend(q, t_pos, kc, vc, n_slc, gather_slc, kw, vw, w_pos, gate, bias_tab):
    bsz, nq, nh, dk = q.shape
    nc = kc.shape[1]
    scale = 1.0 / math.sqrt(dk)
    q32 = q.astype(jnp.float32)
    c_end = jnp.arange(nc) * NSA_CMP_STRIDE + NSA_CMP_LEN - 1
    c_ok = c_end[None, :] <= t_pos[:, None]
    c_bias = jnp.take(bias_tab, t5_bucket(t_pos[:, None] - c_end[None, :]), axis=0).transpose(0, 2, 1)[None]
    lc = jnp.einsum('bqhd,bnd->bqhn', q32, kc.astype(jnp.float32)) * scale + c_bias
    p_c = masked_softmax(lc, c_ok[None, :, None, :])
    o_c = jnp.einsum('bqhn,bnd->bqhd', p_c, vc.astype(jnp.float32))
    ci = jnp.arange(nc)[:, None] * NSA_CMP_STRIDE
    sj = jnp.arange(n_slc)[None, :] * NSA_SLC_BLOCK
    overlap = ((ci < sj + NSA_SLC_BLOCK) & (ci + NSA_CMP_LEN > sj)).astype(jnp.float32)
    imp = jnp.einsum('bqhn,nj->bqj', p_c, overlap)
    cur = t_pos // NSA_SLC_BLOCK
    jj = jnp.arange(n_slc)[None, :]
    s_ok = jj <= cur[:, None]
    forced = (jj == 0) | (jj == cur[:, None]) | (jj == cur[:, None] - 1)
    score = jnp.where(forced[None], jnp.inf, jnp.where(s_ok[None], imp, -jnp.inf))
    n_top = min(NSA_TOPN, n_slc)
    _, sel = lax.top_k(score, n_top)
    sel_ok = sel <= cur[None, :, None]
    pos = (sel[..., None] * NSA_SLC_BLOCK + jnp.arange(NSA_SLC_BLOCK)).reshape(bsz, nq, n_top * NSA_SLC_BLOCK)
    valid = jnp.repeat(sel_ok, NSA_SLC_BLOCK, axis=-1) & (pos <= t_pos[None, :, None])
    ks, vs = gather_slc(pos)
    s_bias = jnp.take(bias_tab, t5_bucket(t_pos[None, :, None] - pos), axis=0).transpose(0, 1, 3, 2)
    ls = jnp.einsum('bqhd,bqnd->bqhn', q32, ks.astype(jnp.float32)) * scale + s_bias
    p_s = masked_softmax(ls, valid[:, :, None, :])
    o_s = jnp.einsum('bqhn,bqnd->bqhd', p_s, vs.astype(jnp.float32))
    d_w = t_pos[:, None] - w_pos[None, :]
    w_ok = (d_w >= 0) & (d_w < NSA_WINDOW) & (w_pos[None, :] >= 0)
    w_bias = jnp.take(bias_tab, t5_bucket(d_w), axis=0).transpose(0, 2, 1)[None]
    lw = jnp.einsum('bqhd,bnd->bqhn', q32, kw.astype(jnp.float32)) * scale + w_bias
    p_w = masked_softmax(lw, w_ok[None, :, None, :])
    o_w = jnp.einsum('bqhn,bnd->bqhd', p_w, vw.astype(jnp.float32))
    g = gate.astype(jnp.float32)
    return (g[..., 0:1] * o_c + g[..., 1:2] * o_s + g[..., 2:3] * o_w).astype(q.dtype)


def nsa_prompt(q, nkv, gate, w1, b1, w2, b2, pos_emb, bias_tab):
    bsz, s, nh, dk = q.shape
    kc = nsa_compress(nsa_subblocks(nkv[:, :, 0], w1[0]), w1[0], b1[0], w2[0], b2[0], pos_emb[0])
    vc = nsa_compress(nsa_subblocks(nkv[:, :, 1], w1[1]), w1[1], b1[1], w2[1], b2[1], pos_emb[1])
    ks, vs = nkv[:, :, 2], nkv[:, :, 3]
    n_slc = -(-s // NSA_SLC_BLOCK)
    bidx = jnp.arange(bsz)[:, None, None]

    def gather_slc(pos):
        return ks[bidx, pos], vs[bidx, pos]

    pad = ((0, 0), (NSA_WINDOW, 0), (0, 0))
    kw_pad, vw_pad = jnp.pad(nkv[:, :, 4], pad), jnp.pad(nkv[:, :, 5], pad)
    nwk = NSA_WINDOW + NSA_Q_BLOCK

    def block(args):
        qb, gb, tb = args
        start = tb[0]
        kwb = lax.dynamic_slice_in_dim(kw_pad, start, nwk, axis=1)
        vwb = lax.dynamic_slice_in_dim(vw_pad, start, nwk, axis=1)
        w_pos = start - NSA_WINDOW + jnp.arange(nwk, dtype=jnp.int32)
        return nsa_attend(qb, tb, kc, vc, n_slc, gather_slc, kwb, vwb, w_pos, gb, bias_tab)

    nqb = s // NSA_Q_BLOCK
    qs = q.reshape(bsz, nqb, NSA_Q_BLOCK, nh, dk).swapaxes(0, 1)
    gs = gate.reshape(bsz, nqb, NSA_Q_BLOCK, nh, 3).swapaxes(0, 1)
    ts = jnp.arange(s, dtype=jnp.int32).reshape(nqb, NSA_Q_BLOCK)
    out = lax.map(block, (qs, gs, ts))
    return out.swapaxes(0, 1).reshape(bsz, s, nh, dk)


def nsa_sample(q, nkv, gate, cache, layer, page_table, win_buf, w1, b1, w2, b2, pos_emb, bias_tab):
    bsz, s, nh, dk = q.shape
    n_pages = page_table.shape[1]
    past = n_pages * PAGE_SIZE
    t_pos = past + jnp.arange(s, dtype=jnp.int32)
    s_pad = -(-s // NSA_CMP_STRIDE) * NSA_CMP_STRIDE

    def compressed(c):
        pool_sub = nsa_subblocks(cache[layer, :, :, c], w1[c])
        past_sub = pool_sub[page_table].reshape(bsz, past // NSA_CMP_STRIDE, 2, NSA_CMP_HIDDEN)
        new_sub = nsa_subblocks(jnp.pad(nkv[:, :, c], ((0, 0), (0, s_pad - s), (0, 0))), w1[c])
        return nsa_compress(jnp.concatenate([past_sub, new_sub], axis=1), w1[c], b1[c], w2[c], b2[c], pos_emb[c])

    kc, vc = compressed(0), compressed(1)
    n_slc = -(-(past + s) // NSA_SLC_BLOCK)

    def gather_slc(pos):
        bidx, page, slot, j, is_past = locate(page_table, pos, s)
        m = is_past[..., None]
        return (jnp.where(m, cache[layer, page, slot, 2], nkv[bidx, j, 2]),
                jnp.where(m, cache[layer, page, slot, 3], nkv[bidx, j, 3]))

    nwb = win_buf.shape[1]
    win_all = jnp.concatenate([win_buf, nkv[:, :, 4:6].astype(win_buf.dtype)], axis=1)
    w_pos = past - nwb + jnp.arange(nwb + s, dtype=jnp.int32)
    out = nsa_attend(q, t_pos, kc, vc, n_slc, gather_slc, win_all[:, :, 0], win_all[:, :, 1], w_pos, gate, bias_tab)
    return out, win_all[:, s:]


def mla_attend(q_lat, q_pe, t_pos, ckv, kpe, k_pos):
    scale = 1.0 / math.sqrt(MLA_NOPE + MLA_ROPE)
    ckv32 = ckv.astype(jnp.float32)
    logits = (jnp.einsum('bqhr,blr->bqhl', q_lat.astype(jnp.float32), ckv32)
              + jnp.einsum('bqhp,blp->bqhl', q_pe.astype(jnp.float32), kpe.astype(jnp.float32))) * scale
    mask = k_pos[None, :] <= t_pos[:, None]
    p = masked_softmax(logits, mask[None, :, None, :])
    return jnp.einsum('bqhl,blr->bqhr', p, ckv32)


def mla_prompt(q_lat, q_pe, mla_row):
    bsz, s, nh, r = q_lat.shape
    ckv, kpe = mla_row[..., :MLA_KV_RANK], mla_row[..., MLA_KV_RANK:]
    k_pos = jnp.arange(s, dtype=jnp.int32)
    nqb = s // MLA_Q_BLOCK
    qls = q_lat.reshape(bsz, nqb, MLA_Q_BLOCK, nh, r).swapaxes(0, 1)
    qps = q_pe.reshape(bsz, nqb, MLA_Q_BLOCK, nh, MLA_ROPE).swapaxes(0, 1)
    ts = k_pos.reshape(nqb, MLA_Q_BLOCK)
    out = lax.map(lambda a: mla_attend(a[0], a[1], a[2], ckv, kpe, k_pos), (qls, qps, ts))
    return out.swapaxes(0, 1).reshape(bsz, s, nh, r)


def mla_sample(q_lat, q_pe, mla_new, cache, layer, page_table):
    bsz, s = q_lat.shape[:2]
    past = page_table.shape[1] * PAGE_SIZE
    rows = jnp.concatenate([cache[layer, page_table].reshape(bsz, past, MLA_KV_RANK + MLA_ROPE),
                            mla_new.astype(cache.dtype)], axis=1)
    t_pos = past + jnp.arange(s, dtype=jnp.int32)
    k_pos = jnp.arange(past + s, dtype=jnp.int32)
    return mla_attend(q_lat, q_pe, t_pos, rows[..., :MLA_KV_RANK], rows[..., MLA_KV_RANK:], k_pos)


def mixer_inputs(x, t_pos, norm_g, w_in_l, q_norm_l, kv_norm_l, w_q_up_l, w_kv_up_l):
    bsz, s, _ = x.shape
    xn = rms_norm(x, norm_g)
    z = xn @ w_in_l
    splits = [int(v) for v in np.cumsum(IN_SIZES)[:-1]]
    mq, mk, mv, nq, nkv, ng, cq, ckv, kpe, mg = jnp.split(z, splits, axis=-1)
    mq = mq.reshape(bsz, s, MOBA_HEADS, HEAD_DIM)
    mk = mk.reshape(bsz, s, MOBA_HEADS, HEAD_DIM)
    mv = mv.reshape(bsz, s, MOBA_HEADS, HEAD_DIM)
    nq = nq.reshape(bsz, s, NSA_HEADS, NSA_DK)
    nkv = nkv.reshape(bsz, s, 6, NSA_DK)
    ng = jax.nn.sigmoid(ng.reshape(bsz, s, NSA_HEADS, 3).astype(jnp.float32))
    qh = (rms_norm(cq, q_norm_l) @ w_q_up_l).reshape(bsz, s, MLA_HEADS, MLA_NOPE + MLA_ROPE)
    q_pe = rope(qh[..., MLA_NOPE:], t_pos)
    q_lat = jnp.einsum('bshn,rhn->bshr', qh[..., :MLA_NOPE], w_kv_up_l[..., :MLA_NOPE])
    mla_row = jnp.concatenate([rms_norm(ckv, kv_norm_l), rope(kpe, t_pos)], axis=-1)
    mg = mg.reshape(bsz, s, 3, D_MODEL)
    return mq, mk, mv, nq, nkv, ng, q_lat, q_pe, mla_row, mg


def mixer_output_ffn(x, o_moba, o_nsa, o_lat, mg, p_l, w_kv_up_l, w_branch_l, w_out_l, norm_ffn_l,
                     w_gate_l, w_up_l, w_down_l, norm_ple_l, w_ple_gate_l, w_ple_proj_l):
    bsz, s, _ = x.shape
    o_mla = jnp.einsum('bshr,rhv->bshv', o_lat, w_kv_up_l[..., MLA_NOPE:].astype(jnp.float32)).astype(x.dtype)
    b_moba = o_moba.reshape(bsz, s, MOBA_W) @ w_branch_l[:MOBA_W]
    b_nsa = o_nsa.reshape(bsz, s, NSA_W) @ w_branch_l[MOBA_W:MOBA_W + NSA_W]
    b_mla = o_mla.reshape(bsz, s, MLA_W) @ w_branch_l[MOBA_W + NSA_W:]
    g = jax.nn.sigmoid(mg)
    h = g[:, :, 0] * b_moba + g[:, :, 1] * b_nsa + g[:, :, 2] * b_mla
    x = x + h @ w_out_l
    xn = rms_norm(x, norm_ffn_l)
    x = x + (jax.nn.silu(xn @ w_gate_l) * (xn @ w_up_l)) @ w_down_l
    gp = jax.nn.sigmoid(rms_norm(x, norm_ple_l) @ w_ple_gate_l)
    return x + gp * (p_l @ w_ple_proj_l)


def setup_inputs(seed: int = 0) -> dict:
    key = jax.random.key(seed)
    ks = iter(jax.random.split(key, 32))
    n_pages = PAST_LEN // PAGE_SIZE
    n_used = DEC_BATCH * n_pages
    n_pool = n_used + n_used // 4
    win_buf = min(NSA_WINDOW, PAST_LEN)

    def nrm(shape, scale):
        return jax.random.normal(next(ks), shape, jnp.float32) * scale

    def gain(shape):
        return 1.0 + nrm(shape, 0.05)

    d = {}
    d['x_prompt'] = nrm((BATCH, SEQ, D_MODEL), 1.0)
    d['x_sample'] = nrm((DEC_BATCH, DEC_SEQ, D_MODEL), 1.0)
    d['cache_moba_kv'] = nrm((DEPTH, n_pool, PAGE_SIZE, 2, MOBA_HEADS, HEAD_DIM), 1.0)
    d['cache_nsa_kv'] = nrm((DEPTH, n_pool, PAGE_SIZE, 4, NSA_DK), 1.0)
    d['cache_mla'] = nrm((DEPTH, n_pool, PAGE_SIZE, MLA_KV_RANK + MLA_ROPE), 1.0)
    d['state_nsa_win'] = nrm((DEPTH, DEC_BATCH, win_buf, 2, NSA_DK), 1.0)
    d['page_table'] = jax.random.permutation(next(ks), n_pool)[:n_used].reshape(DEC_BATCH, n_pages).astype(jnp.int32)
    d['p_prompt'] = nrm((DEPTH, BATCH, SEQ, PLE_DIM), 1.0)
    d['p_sample'] = nrm((DEPTH, DEC_BATCH, DEC_SEQ, PLE_DIM), 1.0)
    d['t5_bias'] = nrm((T5_BUCKETS, MOBA_HEADS + NSA_HEADS), 0.5)
    d['norm_mix'] = gain((DEPTH, D_MODEL))
    d['w_in'] = nrm((DEPTH, D_MODEL, N_IN), D_MODEL ** -0.5)
    d['mla_q_norm'] = gain((DEPTH, MLA_Q_RANK))
    d['mla_kv_norm'] = gain((DEPTH, MLA_KV_RANK))
    d['w_q_up'] = nrm((DEPTH, MLA_Q_RANK, MLA_HEADS * (MLA_NOPE + MLA_ROPE)), MLA_Q_RANK ** -0.5)
    d['w_kv_up'] = nrm((DEPTH, MLA_KV_RANK, MLA_HEADS, MLA_NOPE + MLA_V), MLA_KV_RANK ** -0.5)
    d['nsa_cmp_w1'] = nrm((DEPTH, 2, NSA_CMP_LEN, NSA_DK, NSA_CMP_HIDDEN), (NSA_CMP_LEN * NSA_DK) ** -0.5)
    d['nsa_cmp_b1'] = nrm((DEPTH, 2, NSA_CMP_HIDDEN), 0.02)
    d['nsa_cmp_w2'] = nrm((DEPTH, 2, NSA_CMP_HIDDEN, NSA_DK), NSA_CMP_HIDDEN ** -0.5)
    d['nsa_cmp_b2'] = nrm((DEPTH, 2, NSA_DK), 0.02)
    d['nsa_cmp_pos'] = nrm((DEPTH, 2, NSA_CMP_LEN, NSA_DK), 0.1)
    d['w_branch'] = nrm((DEPTH, MIX_WIDTH, D_MODEL), MOBA_W ** -0.5)
    d['w_out'] = nrm((DEPTH, D_MODEL, D_MODEL), D_MODEL ** -0.5)
    d['norm_ffn'] = gain((DEPTH, D_MODEL))
    d['w_ffn_gate'] = nrm((DEPTH, D_MODEL, D_FF), D_MODEL ** -0.5)
    d['w_ffn_up'] = nrm((DEPTH, D_MODEL, D_FF), D_MODEL ** -0.5)
    d['w_ffn_down'] = nrm((DEPTH, D_FF, D_MODEL), D_FF ** -0.5)
    d['norm_ple'] = gain((DEPTH, D_MODEL))
    d['w_ple_gate'] = nrm((DEPTH, D_MODEL, D_MODEL), D_MODEL ** -0.5)
    d['w_ple_proj'] = nrm((DEPTH, PLE_DIM, D_MODEL), PLE_DIM ** -0.5)
    d['norm_final'] = gain((D_MODEL,))
    return d


def reference(x_prompt, x_sample, cache_moba_kv, cache_nsa_kv, cache_mla, state_nsa_win, page_table,
              p_prompt, p_sample, t5_bias, norm_mix, w_in, mla_q_norm, mla_kv_norm, w_q_up, w_kv_up,
              nsa_cmp_w1, nsa_cmp_b1, nsa_cmp_w2, nsa_cmp_b2, nsa_cmp_pos, w_branch, w_out, norm_ffn,
              w_ffn_gate, w_ffn_up, w_ffn_down, norm_ple, w_ple_gate, w_ple_proj, norm_final):
    seq = x_prompt.shape[1]
    dec_seq = x_sample.shape[1]
    past = page_table.shape[1] * PAGE_SIZE
    t_prompt = jnp.arange(seq, dtype=jnp.int32)
    t_sample = past + jnp.arange(dec_seq, dtype=jnp.int32)
    wp = min(NSA_WINDOW, seq)
    bias_moba = t5_bias[:, :MOBA_HEADS]
    bias_nsa = t5_bias[:, MOBA_HEADS:]
    xp, xs = x_prompt, x_sample
    moba_p, moba_s, nsa_p, nsa_s, mla_p, mla_s, win_p, win_s = [], [], [], [], [], [], [], []
    for l in range(DEPTH):
        cmp_args = (nsa_cmp_w1[l], nsa_cmp_b1[l], nsa_cmp_w2[l], nsa_cmp_b2[l], nsa_cmp_pos[l])
        out_args = (w_kv_up[l], w_branch[l], w_out[l], norm_ffn[l], w_ffn_gate[l], w_ffn_up[l],
                    w_ffn_down[l], norm_ple[l], w_ple_gate[l], w_ple_proj[l])
        mq, mk, mv, nq, nkv, ng, q_lat, q_pe, mla_row, mg = mixer_inputs(
            xp, t_prompt, norm_mix[l], w_in[l], mla_q_norm[l], mla_kv_norm[l], w_q_up[l], w_kv_up[l])
        o_m = moba_prompt(mq, mk, mv, bias_moba)
        o_n = nsa_prompt(nq, nkv, ng, *cmp_args, bias_nsa)
        o_l = mla_prompt(q_lat, q_pe, mla_row)
        xp = mixer_output_ffn(xp, o_m, o_n, o_l, mg, p_prompt[l], *out_args)
        moba_p.append(jnp.stack([mk, mv], axis=2))
        nsa_p.append(nkv[:, :, :4])
        mla_p.append(mla_row)
        win_p.append(nkv[:, seq - wp:, 4:6])
        mq, mk, mv, nq, nkv, ng, q_lat, q_pe, mla_row, mg = mixer_inputs(
            xs, t_sample, norm_mix[l], w_in[l], mla_q_norm[l], mla_kv_norm[l], w_q_up[l], w_kv_up[l])
        o_m = moba_sample(mq, mk, mv, cache_moba_kv, l, page_table, bias_moba)
        o_n, win_new = nsa_sample(nq, nkv, ng, cache_nsa_kv, l, page_table, state_nsa_win[l], *cmp_args, bias_nsa)
        o_l = mla_sample(q_lat, q_pe, mla_row, cache_mla, l, page_table)
        xs = mixer_output_ffn(xs, o_m, o_n, o_l, mg, p_sample[l], *out_args)
        moba_s.append(jnp.stack([mk, mv], axis=2))
        nsa_s.append(nkv[:, :, :4])
        mla_s.append(mla_row)
        win_s.append(win_new)
    y_prompt = rms_norm(xp, norm_final)
    y_sample = rms_norm(xs, norm_final)
    return (y_prompt, y_sample, jnp.stack(moba_p), jnp.stack(moba_s), jnp.stack(nsa_p), jnp.stack(nsa_s),
            jnp.stack(mla_p), jnp.stack(mla_s), jnp.stack(win_p), jnp.stack(win_s))
```

```python
import functools
import math

import numpy as np
import jax
import jax.numpy as jnp
from jax import lax
from jax.experimental import pallas as pl
from jax.experimental.pallas import tpu as pltpu

PAGE_SIZE = 128
HEAD_DIM = 64
MOBA_HEADS = 4
MOBA_BLOCK = 256
MOBA_TOPK = 3
NSA_HEADS = 4
NSA_DK = 64
NSA_CMP_LEN = 32
NSA_CMP_STRIDE = 16
NSA_CMP_HIDDEN = 128
NSA_SLC_BLOCK = 64
NSA_TOPN = 16
NSA_WINDOW = 512
MLA_HEADS = 8
MLA_Q_RANK = 256
MLA_KV_RANK = 128
MLA_NOPE = 64
MLA_ROPE = 32
MLA_V = 64
ROPE_THETA = 10000.0
T5_BUCKETS = 32
T5_MAX_DIST = 128
EPS = 1e-6

LANES = 128
ATT_TILE = 256
VMEM_LIMIT = 56 * 1024 * 1024

C_MQ, C_MK, C_MV, C_NQ, C_NKV, C_NG = 0, 256, 512, 768, 1024, 1408
C_CQ, C_CKV, C_KPE, C_KPER, C_MG, N_PAD = 1536, 1792, 1920, 1952, 2048, 5120

BF16 = jnp.bfloat16
F32 = jnp.float32
NEG_INF = float("-inf")


def _cparams(*sem):
    return pltpu.CompilerParams(dimension_semantics=sem, vmem_limit_bytes=VMEM_LIMIT)


def _rms(x, g):
    return x * lax.rsqrt(jnp.mean(x * x, axis=-1, keepdims=True) + EPS) * g


def _dot(a, b):
    return jnp.dot(a, b, preferred_element_type=F32)


def _dot_nt(a, b, precision=None):
    return lax.dot_general(a, b, (((1,), (1,)), ((), ())), preferred_element_type=F32, precision=precision)


def _masked_softmax(s, mask):
    s = jnp.where(mask, s, NEG_INF)
    m = jnp.max(s, axis=-1, keepdims=True)
    m = jnp.where(m == NEG_INF, 0.0, m)
    e = jnp.exp(s - m)
    return e / jnp.maximum(jnp.sum(e, axis=-1, keepdims=True), 1e-30)


def _rank_before(score, n):
    lane = lax.broadcasted_iota(jnp.int32, score.shape, score.ndim - 1)
    rank = jnp.zeros(score.shape, jnp.int32)
    for m in range(n):
        sm = score[..., m:m + 1]
        before = (sm > score) | ((sm == score) & (m < lane))
        rank = rank + before.astype(jnp.int32)
    return rank


def _flash_update(s, v, m_ref, l_ref, acc_ref):
    m_prev = m_ref[...]
    m_new = jnp.maximum(m_prev, jnp.max(s, axis=-1, keepdims=True))
    m_safe = jnp.where(m_new == NEG_INF, 0.0, m_new)
    alpha = jnp.exp(m_prev - m_safe)
    p = jnp.exp(s - m_safe[:, :1])
    l_ref[...] = alpha * l_ref[...] + jnp.sum(p, axis=-1, keepdims=True)
    acc_ref[...] = alpha[:, :acc_ref.shape[-1]] * acc_ref[...] + _dot(p.astype(BF16), v)
    m_ref[...] = m_new


def _t5_bucket_np(dist):
    n = np.maximum(dist, 0)
    max_exact = T5_BUCKETS // 2
    nf = np.maximum(n, 1).astype(np.float32)
    large = max_exact + (np.log(nf / max_exact) / math.log(T5_MAX_DIST / max_exact)
                         * (T5_BUCKETS - max_exact)).astype(np.int32)
    return np.where(n < max_exact, n, np.minimum(large, T5_BUCKETS - 1)).astype(np.int32)


def _toeplitz_buckets(tile, n_delta):
    i = np.arange(tile)[:, None]
    j = np.arange(tile)[None, :]
    return np.stack([_t5_bucket_np(d * tile + i - j) for d in range(n_delta)])


def _n_distinct_deltas(tile, n_tiles):
    for n in range(1, n_tiles + 1):
        lo = (n - 1) * tile - (tile - 1)
        if lo >= 0 and np.all(_t5_bucket_np(np.arange(lo, n_tiles * tile)) == _t5_bucket_np(np.array(lo))):
            return n
    return n_tiles


def _inproj_kernel(x_ref, g_ref, w_ref, o_ref):
    xn = _rms(x_ref[...], g_ref[...]).astype(BF16)
    o_ref[...] = _dot(xn, w_ref[...])


def _inproj(x, g, w):
    m = x.shape[0]
    tm = min(256, m)
    return pl.pallas_call(
        _inproj_kernel,
        grid=(m // tm,),
        in_specs=[pl.BlockSpec((tm, x.shape[1]), lambda i: (i, 0)),
                  pl.BlockSpec((1, x.shape[1]), lambda i: (0, 0)),
                  pl.BlockSpec(w.shape, lambda i: (0, 0))],
        out_specs=pl.BlockSpec((tm, w.shape[1]), lambda i: (i, 0)),
        out_shape=jax.ShapeDtypeStruct((m, w.shape[1]), F32),
        compiler_params=_cparams("parallel"),
    )(x, g, w)


def _mla_prep_kernel(cq_ref, ck_ref, qn_ref, kn_ref, wn_ref, wp_ref, wpr_ref, wk_ref, cos_ref, sin_ref,
                     qcat_ref, row_ref, kcat_ref):
    tm = cq_ref.shape[0]
    cqn = _rms(cq_ref[...], qn_ref[...]).astype(BF16)
    q_nope = _dot(cqn, wn_ref[...])
    cos = cos_ref[...]
    sin = sin_ref[...]
    q_pe = _dot(cqn, wp_ref[...]) * cos + _dot(cqn, wpr_ref[...]) * sin
    ck = ck_ref[...]
    ckv_n = _rms(ck[:, :MLA_KV_RANK], kn_ref[...])
    k_pe = (ck[:, MLA_KV_RANK:MLA_KV_RANK + MLA_ROPE] * cos[:, :MLA_ROPE]
            + ck[:, MLA_KV_RANK + MLA_ROPE:MLA_KV_RANK + 2 * MLA_ROPE] * sin[:, :MLA_ROPE])
    row_ref[...] = jnp.concatenate([ckv_n, k_pe], axis=-1)
    pad = jnp.zeros((tm, 2 * LANES - MLA_KV_RANK - MLA_ROPE), F32)
    kcat_ref[...] = jnp.concatenate([ckv_n, k_pe, pad], axis=-1).astype(BF16)
    for h in range(MLA_HEADS):
        q_lat = _dot(q_nope[:, h * MLA_NOPE:(h + 1) * MLA_NOPE].astype(BF16), wk_ref[h])
        qcat_ref[h] = jnp.concatenate([q_lat, q_pe[:, h * MLA_ROPE:(h + 1) * MLA_ROPE], pad], axis=-1).astype(BF16)


def _mla_prep(z, bv, sv, qn, kn, wn, wp, wpr, wk, cos, sin):
    z3 = z.reshape(bv, sv, N_PAD)
    tm = min(256, sv)
    full = lambda a: pl.BlockSpec(a.shape, lambda b, i: (0,) * a.ndim)
    return pl.pallas_call(
        _mla_prep_kernel,
        grid=(bv, sv // tm),
        in_specs=[pl.BlockSpec((None, tm, 256), lambda b, i: (b, i, C_CQ // 256)),
                  pl.BlockSpec((None, tm, 256), lambda b, i: (b, i, C_CKV // 256)),
                  full(qn), full(kn), full(wn), full(wp), full(wpr), full(wk),
                  pl.BlockSpec((tm, 256), lambda b, i: (i, 0)),
                  pl.BlockSpec((tm, 256), lambda b, i: (i, 0))],
        out_specs=[pl.BlockSpec((None, MLA_HEADS, tm, 256), lambda b, i: (b, 0, i, 0)),
                   pl.BlockSpec((None, tm, MLA_KV_RANK + MLA_ROPE), lambda b, i: (b, i, 0)),
                   pl.BlockSpec((None, tm, 256), lambda b, i: (b, i, 0))],
        out_shape=[jax.ShapeDtypeStruct((bv, MLA_HEADS, sv, 256), BF16),
                   jax.ShapeDtypeStruct((bv, sv, MLA_KV_RANK + MLA_ROPE), F32),
                   jax.ShapeDtypeStruct((bv, sv, 256), BF16)],
        compiler_params=_cparams("parallel", "parallel"),
    )(z3, z3, qn, kn, wn, wp, wpr, wk, cos, sin)


def _mixer_out_kernel(om_ref, on_ref, ol_ref, g0_ref, g1_ref, g2_ref, x_ref, wuv_ref, wb_ref, wo_ref, o_ref):
    o_mla = jnp.concatenate([_dot(ol_ref[h], wuv_ref[h]) for h in range(MLA_HEADS)], axis=-1).astype(BF16)
    n_a = om_ref.shape[-1]
    n_b = n_a + on_ref.shape[-1]
    b_moba = _dot(om_ref[...], wb_ref[:n_a, :])
    b_nsa = _dot(on_ref[...], wb_ref[n_a:n_b, :])
    b_mla = _dot(o_mla, wb_ref[n_b:, :])
    h = (jax.nn.sigmoid(g0_ref[...]) * b_moba + jax.nn.sigmoid(g1_ref[...]) * b_nsa
         + jax.nn.sigmoid(g2_ref[...]) * b_mla)
    o_ref[...] = x_ref[...] + _dot(h.astype(BF16), wo_ref[...])


def _mixer_out(x, z, o_moba, o_nsa, o_lat, wuv, wb, wo):
    bv, sv, d = x.shape
    z3 = z.reshape(bv, sv, N_PAD)
    tm = min(256, sv)
    full = lambda a: pl.BlockSpec(a.shape, lambda b, i: (0,) * a.ndim)
    gate = lambda k: pl.BlockSpec((None, tm, d), lambda b, i: (b, i, C_MG // d + k))
    row = lambda w: pl.BlockSpec((None, tm, w), lambda b, i: (b, i, 0))
    return pl.pallas_call(
        _mixer_out_kernel,
        grid=(bv, sv // tm),
        in_specs=[row(o_moba.shape[-1]), row(o_nsa.shape[-1]),
                  pl.BlockSpec((None, MLA_HEADS, tm, MLA_KV_RANK), lambda b, i: (b, 0, i, 0)),
                  gate(0), gate(1), gate(2), row(d), full(wuv), full(wb), full(wo)],
        out_specs=row(d),
        out_shape=jax.ShapeDtypeStruct(x.shape, F32),
        compiler_params=_cparams("parallel", "parallel"),
    )(o_moba, o_nsa, o_lat, z3, z3, z3, x, wuv, wb, wo)


def _ffn_ple_kernel(x_ref, gf_ref, wg_ref, wu_ref, wd_ref, gp_ref, wpg_ref, wpp_ref, p_ref, gl_ref,
                    o_ref, xn_ref, acc_ref, *, final_norm):
    f = pl.program_id(1)

    @pl.when(f == 0)
    def _():
        xn_ref[...] = _rms(x_ref[...], gf_ref[...]).astype(BF16)
        acc_ref[...] = jnp.zeros_like(acc_ref)

    xn = xn_ref[...]
    a = jax.nn.silu(_dot(xn, wg_ref[...])) * _dot(xn, wu_ref[...])
    acc_ref[...] += _dot(a.astype(BF16), wd_ref[...])

    @pl.when(f == pl.num_programs(1) - 1)
    def _():
        x2 = x_ref[...] + acc_ref[...]
        gate = jax.nn.sigmoid(_dot(_rms(x2, gp_ref[...]).astype(BF16), wpg_ref[...]))
        x3 = x2 + gate * _dot(p_ref[...].astype(BF16), wpp_ref[...])
        o_ref[...] = _rms(x3, gl_ref[...]) if final_norm else x3


def _ffn_ple(x, gf, wg, wu, wd, gp, wpg, wpp, p, gl, final_norm):
    m, d = x.shape
    dff = wg.shape[1]
    tm = min(512, m)
    tf = dff // 2 if (dff // 2) % LANES == 0 else dff
    full = lambda a: pl.BlockSpec(a.shape, lambda i, f: (0,) * a.ndim)
    return pl.pallas_call(
        functools.partial(_ffn_ple_kernel, final_norm=final_norm),
        grid=(m // tm, dff // tf),
        in_specs=[pl.BlockSpec((tm, d), lambda i, f: (i, 0)), full(gf),
                  pl.BlockSpec((d, tf), lambda i, f: (0, f)),
                  pl.BlockSpec((d, tf), lambda i, f: (0, f)),
                  pl.BlockSpec((tf, d), lambda i, f: (f, 0)),
                  full(gp), full(wpg), full(wpp),
                  pl.BlockSpec((tm, p.shape[1]), lambda i, f: (i, 0)), full(gl)],
        out_specs=pl.BlockSpec((tm, d), lambda i, f: (i, 0)),
        out_shape=jax.ShapeDtypeStruct((m, d), F32),
        scratch_shapes=[pltpu.VMEM((tm, d), BF16), pltpu.VMEM((tm, d), F32)],
        compiler_params=_cparams("parallel", "arbitrary"),
    )(x, gf, wg, wu, wd, gp, wpg, wpp, p, gl)


def _mla_attn_kernel(q_ref, k_ref, o_ref, m_ref, l_ref, acc_ref, *, tq, tk, scale):
    qi = pl.program_id(1)
    ki = pl.program_id(2)
    rows = MLA_HEADS * tq

    @pl.when(ki == 0)
    def _():
        m_ref[...] = jnp.full_like(m_ref, NEG_INF)
        l_ref[...] = jnp.zeros_like(l_ref)
        acc_ref[...] = jnp.zeros_like(acc_ref)

    @pl.when(ki * tk <= qi * tq + tq - 1)
    def _():
        q = q_ref[...].reshape(rows, q_ref.shape[-1])
        k = k_ref[...]
        s = _dot_nt(q, k) * scale
        t_pos = qi * tq + lax.broadcasted_iota(jnp.int32, (MLA_HEADS, tq, tk), 1).reshape(rows, tk)
        k_pos = ki * tk + lax.broadcasted_iota(jnp.int32, (rows, tk), 1)
        s = jnp.where(k_pos <= t_pos, s, NEG_INF)
        _flash_update(s, k[:, :MLA_KV_RANK], m_ref, l_ref, acc_ref)

    @pl.when(ki == pl.num_programs(2) - 1)
    def _():
        o = acc_ref[...] / l_ref[...][:, :MLA_KV_RANK]
        o_ref[...] = o.reshape(MLA_HEADS, tq, MLA_KV_RANK).astype(o_ref.dtype)


def _mla_attn_prompt(q_cat, k_cat):
    b, nh, s, w = q_cat.shape
    tq, tk = 128, 256
    scale = 1.0 / math.sqrt(MLA_NOPE + MLA_ROPE)
    rows = nh * tq
    return pl.pallas_call(
        functools.partial(_mla_attn_kernel, tq=tq, tk=tk, scale=scale),
        grid=(b, s // tq, s // tk),
        in_specs=[pl.BlockSpec((None, nh, tq, w), lambda b, i, j: (b, 0, i, 0)),
                  pl.BlockSpec((None, tk, w), lambda b, i, j: (b, jnp.minimum(j, (i * tq + tq - 1) // tk), 0))],
        out_specs=pl.BlockSpec((None, nh, tq, MLA_KV_RANK), lambda b, i, j: (b, 0, i, 0)),
        out_shape=jax.ShapeDtypeStruct((b, nh, s, MLA_KV_RANK), BF16),
        scratch_shapes=[pltpu.VMEM((rows, LANES), F32), pltpu.VMEM((rows, LANES), F32),
                        pltpu.VMEM((rows, MLA_KV_RANK), F32)],
        compiler_params=_cparams("parallel", "parallel", "arbitrary"),
    )(q_cat, k_cat)


def _moba_prompt_kernel(q_ref, k_ref, v_ref, bias_ref, o_ref, kmean_ref, m_ref, l_ref, acc_ref, *, nb, n_delta):
    t = ATT_TILE
    qi = pl.program_id(1)
    width = MOBA_HEADS * HEAD_DIM

    @pl.when(qi == 0)
    def _():
        for n in range(nb):
            kmean_ref[n:n + 1, :] = jnp.mean(k_ref[n * t:(n + 1) * t, :], axis=0, keepdims=True)

    q = q_ref[...]
    lane = lax.broadcasted_iota(jnp.int32, (t, width), 1)
    blk = lax.broadcasted_iota(jnp.int32, (t, nb), 1)
    row_i = lax.broadcasted_iota(jnp.int32, (t, t), 0)
    col_i = lax.broadcasted_iota(jnp.int32, (t, t), 1)
    lane2 = lax.broadcasted_iota(jnp.int32, (t, 2 * HEAD_DIM), 1)
    kmean = kmean_ref[...]
    outs = []
    for h in range(MOBA_HEADS):
        q_h = jnp.where((lane >= h * HEAD_DIM) & (lane < (h + 1) * HEAD_DIM), q, 0.0)
        gate = _dot_nt(q_h, kmean, precision=lax.Precision.HIGHEST)
        score = jnp.where(blk < qi, gate, NEG_INF)
        sel = (_rank_before(score, nb) < MOBA_TOPK) & (blk < qi)
        hp, half = h // 2, h % 2
        q2 = q[:, hp * 2 * HEAD_DIM:(hp + 1) * 2 * HEAD_DIM]
        q2 = jnp.where((lane2 >= half * HEAD_DIM) & (lane2 < (half + 1) * HEAD_DIM), q2, 0.0).astype(BF16)
        m_ref[...] = jnp.full_like(m_ref, NEG_INF)
        l_ref[...] = jnp.zeros_like(l_ref)
        acc_ref[...] = jnp.zeros_like(acc_ref)
        for n in range(nb):
            @pl.when(n <= qi)
            def _(n=n, sel=sel, q2=q2, hp=hp, h=h):
                kb = k_ref[n * t:(n + 1) * t, hp * 2 * HEAD_DIM:(hp + 1) * 2 * HEAD_DIM].astype(BF16)
                vb = v_ref[n * t:(n + 1) * t, hp * 2 * HEAD_DIM:(hp + 1) * 2 * HEAD_DIM].astype(BF16)
                s = _dot_nt(q2, kb) * (1.0 / math.sqrt(HEAD_DIM)) + bias_ref[jnp.minimum(qi - n, n_delta - 1), h]
                picked = jnp.sum(jnp.where(blk == n, sel.astype(F32), 0.0), axis=-1, keepdims=True)
                mask = jnp.where(n == qi, (col_i <= row_i).astype(F32), picked)
                _flash_update(jnp.where(mask > 0.5, s, NEG_INF), vb, m_ref, l_ref, acc_ref)
        o2 = acc_ref[...] / l_ref[...]
        outs.append(o2[:, half * HEAD_DIM:(half + 1) * HEAD_DIM])
    o_ref[...] = jnp.concatenate(outs, axis=-1).astype(o_ref.dtype)


def _moba_prompt(z, b, s, bias_tiles):
    assert s % ATT_TILE == 0 and MOBA_BLOCK == ATT_TILE
    z3 = z.reshape(b, s, N_PAD)
    nb = s // MOBA_BLOCK
    width = MOBA_HEADS * HEAD_DIM
    return pl.pallas_call(
        functools.partial(_moba_prompt_kernel, nb=nb, n_delta=bias_tiles.shape[0]),
        grid=(b, nb),
        in_specs=[pl.BlockSpec((None, ATT_TILE, width), lambda b, i: (b, i, C_MQ // width)),
                  pl.BlockSpec((None, s, width), lambda b, i: (b, 0, C_MK // width)),
                  pl.BlockSpec((None, s, width), lambda b, i: (b, 0, C_MV // width)),
                  pl.BlockSpec(bias_tiles.shape, lambda b, i: (0, 0, 0, 0))],
        out_specs=pl.BlockSpec((None, ATT_TILE, width), lambda b, i: (b, i, 0)),
        out_shape=jax.ShapeDtypeStruct((b, s, width), BF16),
        scratch_shapes=[pltpu.VMEM((nb, width), F32), pltpu.VMEM((ATT_TILE, LANES), F32),
                        pltpu.VMEM((ATT_TILE, LANES), F32), pltpu.VMEM((ATT_TILE, LANES), F32)],
        compiler_params=_cparams("parallel", "arbitrary"),
    )(z3, z3, z3, bias_tiles)


def _hpos(pos_ref, w1f_ref, c):
    pos = jnp.broadcast_to(pos_ref[c], (8, pos_ref.shape[-1]))
    return _dot(pos, w1f_ref[c])[0:1, :]


def _nsa_compress_prompt_kernel(z_ref, w16_ref, pos_ref, w1f_ref, b1_ref, w2_ref, b2_ref, o_ref, *, nsub):
    hid = NSA_CMP_HIDDEN
    hs = jnp.zeros((nsub, 4 * hid), F32)
    for j in range(NSA_CMP_STRIDE):
        r = z_ref[pl.ds(j, nsub, stride=NSA_CMP_STRIDE), :]
        hs = hs + _dot(r.astype(BF16), w16_ref[j])
    outs = []
    for c in range(2):
        first = hs[:, 2 * c * hid:(2 * c + 1) * hid]
        second = pltpu.roll(hs[:, (2 * c + 1) * hid:(2 * c + 2) * hid], nsub - 1, 0)
        h = first + second + _hpos(pos_ref, w1f_ref, c) + b1_ref[c]
        outs.append(_dot(jax.nn.silu(h).astype(BF16), w2_ref[c]) + b2_ref[c])
    o_ref[...] = jnp.concatenate(outs, axis=-1)


def _nsa_compress_prompt(z, b, s, w16, posf, w1f, b1, w2, b2):
    z3 = z.reshape(b, s, N_PAD)
    nsub = s // NSA_CMP_STRIDE
    full = lambda a: pl.BlockSpec(a.shape, lambda i: (0,) * a.ndim)
    return pl.pallas_call(
        functools.partial(_nsa_compress_prompt_kernel, nsub=nsub),
        grid=(b,),
        in_specs=[pl.BlockSpec((None, s, LANES), lambda i: (i, 0, C_NKV // LANES)),
                  full(w16), full(posf), full(w1f), full(b1), full(w2), full(b2)],
        out_specs=pl.BlockSpec((None, nsub, 2 * NSA_DK), lambda i: (i, 0, 0)),
        out_shape=jax.ShapeDtypeStruct((b, nsub, 2 * NSA_DK), F32),
        compiler_params=_cparams("parallel"),
    )(z3, w16, posf, w1f, b1, w2, b2)


def _nsa_prompt_kernel(q_ref, g_ref, kvc_ref, slc_ref, win_ref, cbias_ref, bias_ref, ov_ref, o_ref,
                       m_ref, l_ref, acc_ref, *, nt, n_delta, n_slc):
    t = ATT_TILE
    nh = NSA_HEADS
    rows = nh * t
    qi = pl.program_id(1)
    scale = 1.0 / math.sqrt(NSA_DK)
    q = q_ref[...]
    zero = jnp.zeros((t, NSA_DK), F32)
    q4 = jnp.concatenate([jnp.concatenate([q[:, h * NSA_DK:(h + 1) * NSA_DK], zero], axis=-1)
                          for h in range(nh)], axis=0).astype(BF16)
    t_pos = qi * t + lax.broadcasted_iota(jnp.int32, (t, 1), 0)
    t_pos4 = qi * t + lax.broadcasted_iota(jnp.int32, (nh, t, 1), 1).reshape(rows, 1)

    kvc = kvc_ref[...].astype(BF16)
    nc = kvc.shape[0]
    lc = _dot_nt(q4, kvc) * scale + cbias_ref[...].reshape(rows, nc)
    c_end = lax.broadcasted_iota(jnp.int32, (rows, nc), 1) * NSA_CMP_STRIDE + NSA_CMP_LEN - 1
    p_c = _masked_softmax(lc, c_end <= t_pos4)
    o_c = _dot(p_c.astype(BF16), kvc)
    p_sum = p_c[0:t]
    for h in range(1, nh):
        p_sum = p_sum + p_c[h * t:(h + 1) * t]
    imp = jnp.dot(p_sum, ov_ref[...], preferred_element_type=F32, precision=lax.Precision.HIGHEST)

    jj = lax.broadcasted_iota(jnp.int32, (t, n_slc), 1)
    cur = t_pos // NSA_SLC_BLOCK
    s_ok = jj <= cur
    forced = (jj == 0) | (jj == cur) | (jj == cur - 1)
    score = jnp.where(forced, jnp.inf, jnp.where(s_ok, imp, NEG_INF))
    sel = ((_rank_before(score, n_slc) < min(NSA_TOPN, n_slc)) & s_ok).astype(F32)

    row_i = lax.broadcasted_iota(jnp.int32, (t, t), 0)
    col_i = lax.broadcasted_iota(jnp.int32, (t, t), 1)
    per_tile = t // NSA_SLC_BLOCK

    def reset():
        m_ref[...] = jnp.full_like(m_ref, NEG_INF)
        l_ref[...] = jnp.zeros_like(l_ref)
        acc_ref[...] = jnp.zeros_like(acc_ref)

    def logits(src_ref, n):
        kb = src_ref[n * t:(n + 1) * t, :].astype(BF16)
        bias = bias_ref[jnp.minimum(qi - n, n_delta - 1)].reshape(rows, t)
        return _dot_nt(q4, kb) * scale + bias, kb

    reset()
    for n in range(nt):
        @pl.when(n <= qi)
        def _(n=n):
            s, kb = logits(slc_ref, n)
            picked = jnp.zeros((t, t), F32)
            for c in range(per_tile):
                col = jnp.sum(jnp.where(jj == n * per_tile + c, sel, 0.0), axis=-1, keepdims=True)
                picked = jnp.where(col_i // NSA_SLC_BLOCK == c, col, picked)
            mask = jnp.where(n * t + col_i <= qi * t + row_i, picked, 0.0)
            s = jnp.where(jnp.concatenate([mask] * nh, axis=0) > 0.5, s, NEG_INF)
            _flash_update(s, kb, m_ref, l_ref, acc_ref)
    o_s = acc_ref[...] / l_ref[...]

    reset()
    for n in range(nt):
        @pl.when((n <= qi) & ((qi - n) * t < NSA_WINDOW + t - 1))
        def _(n=n):
            s, kb = logits(win_ref, n)
            d = (qi * t + row_i) - (n * t + col_i)
            mask = jnp.where((d >= 0) & (d < NSA_WINDOW), 1.0, 0.0)
            s = jnp.where(jnp.concatenate([mask] * nh, axis=0) > 0.5, s, NEG_INF)
            _flash_update(s, kb, m_ref, l_ref, acc_ref)
    o_w = acc_ref[...] / l_ref[...]

    g = jax.nn.sigmoid(g_ref[...])
    outs = []
    for h in range(nh):
        sl = slice(h * t, (h + 1) * t)
        o_h = (g[:, 3 * h:3 * h + 1] * o_c[sl] + g[:, 3 * h + 1:3 * h + 2] * o_s[sl]
               + g[:, 3 * h + 2:3 * h + 3] * o_w[sl])
        outs.append(o_h[:, NSA_DK:])
    o_ref[...] = jnp.concatenate(outs, axis=-1).astype(o_ref.dtype)


def _nsa_prompt(z, b, s, kvc, cbias, bias_tiles, overlap):
    assert s % ATT_TILE == 0
    z3 = z.reshape(b, s, N_PAD)
    nt = s // ATT_TILE
    nc = kvc.shape[1]
    width = NSA_HEADS * NSA_DK
    full = lambda a: pl.BlockSpec(a.shape, lambda b, i: (0,) * a.ndim)
    seq = lambda col: pl.BlockSpec((None, s, LANES), lambda b, i: (b, 0, col // LANES))
    rows = NSA_HEADS * ATT_TILE
    return pl.pallas_call(
        functools.partial(_nsa_prompt_kernel, nt=nt, n_delta=bias_tiles.shape[0], n_slc=overlap.shape[1]),
        grid=(b, nt),
        in_specs=[pl.BlockSpec((None, ATT_TILE, width), lambda b, i: (b, i, C_NQ // width)),
                  pl.BlockSpec((None, ATT_TILE, LANES), lambda b, i: (b, i, C_NG // LANES)),
                  pl.BlockSpec((None, nc, 2 * NSA_DK), lambda b, i: (b, 0, 0)),
                  seq(C_NKV + 2 * NSA_DK), seq(C_NKV + 4 * NSA_DK),
                  pl.BlockSpec((NSA_HEADS, ATT_TILE, nc), lambda b, i: (0, i, 0)),
                  full(bias_tiles), full(overlap)],
        out_specs=pl.BlockSpec((None, ATT_TILE, width), lambda b, i: (b, i, 0)),
        out_shape=jax.ShapeDtypeStruct((b, s, width), BF16),
        scratch_shapes=[pltpu.VMEM((rows, LANES), F32), pltpu.VMEM((rows, LANES), F32),
                        pltpu.VMEM((rows, LANES), F32)],
        compiler_params=_cparams("parallel", "parallel"),
    )(z3, z3, kvc, z3, z3, cbias, bias_tiles, overlap)


def _page_specs(n, layer, shape, first_page):
    def spec(i):
        return pl.BlockSpec((None, None) + shape, lambda b, *a: (layer, a[-1][b, first_page(*a[:-1]) + i], 0, 0))
    return [spec(i) for i in range(n)]


def _mla_sample_kernel(pt_ref, *refs, n_pages, scale):
    pages = refs[:n_pages]
    q_ref, knew_ref, o_ref, kbuf_ref = refs[n_pages:]
    width = MLA_KV_RANK + MLA_ROPE
    pad = jnp.zeros((PAGE_SIZE, kbuf_ref.shape[-1] - width), F32)
    for i in range(n_pages):
        kbuf_ref[i * PAGE_SIZE:(i + 1) * PAGE_SIZE, :] = jnp.concatenate([pages[i][...], pad], axis=-1).astype(BF16)
    q = q_ref[...]
    k_new = knew_ref[...]
    kbuf = kbuf_ref[...]
    s = _dot_nt(q, kbuf) * scale
    s_new = jnp.sum(q[:, :width].astype(F32) * k_new, axis=-1, keepdims=True) * scale
    m = jnp.maximum(jnp.max(s, axis=-1, keepdims=True), s_new)
    p = jnp.exp(s - m)
    p_new = jnp.exp(s_new - m)
    den = jnp.sum(p, axis=-1, keepdims=True) + p_new
    o = _dot(p.astype(BF16), kbuf[:, :MLA_KV_RANK]) + p_new * k_new[:, :MLA_KV_RANK]
    o_ref[...] = (o / den).astype(o_ref.dtype)


def _mla_sample(cache, layer, page_table, q_cat, row_new):
    b, n_pages = page_table.shape
    width = cache.shape[-1]
    scale = 1.0 / math.sqrt(MLA_NOPE + MLA_ROPE)
    grid_spec = pltpu.PrefetchScalarGridSpec(
        num_scalar_prefetch=1, grid=(b,),
        in_specs=_page_specs(n_pages, layer, (PAGE_SIZE, width), lambda: 0)
        + [pl.BlockSpec((None, MLA_HEADS, q_cat.shape[-1]), lambda i, pt: (i, 0, 0)),
           pl.BlockSpec((None, 1, width), lambda i, pt: (i, 0, 0))],
        out_specs=pl.BlockSpec((None, MLA_HEADS, MLA_KV_RANK), lambda i, pt: (i, 0, 0)),
        scratch_shapes=[pltpu.VMEM((n_pages * PAGE_SIZE, q_cat.shape[-1]), BF16)])
    return pl.pallas_call(
        functools.partial(_mla_sample_kernel, n_pages=n_pages, scale=scale),
        grid_spec=grid_spec,
        out_shape=jax.ShapeDtypeStruct((b, MLA_HEADS, MLA_KV_RANK), BF16),
        compiler_params=_cparams("arbitrary"),
    )(page_table, *([cache] * n_pages), q_cat, row_new)


def _moba_sample_kernel(pt_ref, *refs, n_pages, n_blocks):
    pages = refs[:n_pages]
    q_ref, kvnew_ref, bias_ref, bias0_ref, o_ref, m_ref, l_ref, g_ref, acc_ref = refs[n_pages:]
    c = pl.program_id(1)
    width = MOBA_HEADS * HEAD_DIM
    per_blk = MOBA_BLOCK // PAGE_SIZE
    blk_per_step = n_pages // per_blk
    q = q_ref[...]
    qb = q.astype(BF16)
    lane = lax.broadcasted_iota(jnp.int32, (8, LANES), 1)

    @pl.when(c == 0)
    def _():
        m_ref[...] = jnp.full_like(m_ref, NEG_INF)
        l_ref[...] = jnp.zeros_like(l_ref)
        g_ref[...] = jnp.full_like(g_ref, NEG_INF)

    for j in range(blk_per_step):
        kv = jnp.concatenate([pages[per_blk * j + u][...] for u in range(per_blk)], axis=0)
        k = kv[:, :width]
        v = kv[:, width:]
        n = c * blk_per_step + j
        gate = jnp.sum(q * (jnp.sum(k, axis=0, keepdims=True) * (1.0 / MOBA_BLOCK)), axis=-1, keepdims=True)
        s = _dot_nt(qb, k.astype(BF16)) * (1.0 / math.sqrt(HEAD_DIM)) + bias_ref[:, j * MOBA_BLOCK:(j + 1) * MOBA_BLOCK]
        m = jnp.max(s, axis=-1, keepdims=True)
        p = jnp.exp(s - m)
        m_ref[...] = jnp.where(lane == n, m, m_ref[...])
        l_ref[...] = jnp.where(lane == n, jnp.sum(p, axis=-1, keepdims=True), l_ref[...])
        g_ref[...] = jnp.where(lane == n, gate, g_ref[...])
        acc_ref[n] = _dot(p.astype(BF16), v.astype(BF16))

    @pl.when(c == pl.num_programs(1) - 1)
    def _():
        kv_new = kvnew_ref[...]
        s_self = (jnp.sum(q * kv_new[:, :width], axis=-1, keepdims=True) * (1.0 / math.sqrt(HEAD_DIM))
                  + bias0_ref[...][:, :1])
        past = lane < n_blocks
        score = jnp.where(past, g_ref[...], NEG_INF)
        sel = (_rank_before(score, n_blocks) < MOBA_TOPK) & past
        m_all = m_ref[...]
        m_tot = jnp.maximum(jnp.max(jnp.where(sel, m_all, NEG_INF), axis=-1, keepdims=True), s_self)
        w = jnp.where(sel, jnp.exp(m_all - m_tot), 0.0)
        e_self = jnp.exp(s_self - m_tot)
        den = jnp.sum(w * l_ref[...], axis=-1, keepdims=True) + e_self
        num = e_self * kv_new[:, width:]
        for n in range(n_blocks):
            num = num + w[:, n:n + 1] * acc_ref[n]
        o = num / den
        row = lax.broadcasted_iota(jnp.int32, (8, width), 0)
        lane_w = lax.broadcasted_iota(jnp.int32, (8, width), 1)
        o_ref[...] = jnp.sum(jnp.where(lane_w // HEAD_DIM == row, o, 0.0), axis=0, keepdims=True).astype(o_ref.dtype)


def _moba_sample(cache, layer, page_table, q_bd, kv_new, bias, bias0):
    b, n_pages = page_table.shape
    n_pool = cache.shape[1]
    cache4 = cache.reshape(cache.shape[0], n_pool, PAGE_SIZE, -1)
    width = MOBA_HEADS * HEAD_DIM
    past = n_pages * PAGE_SIZE
    assert past % MOBA_BLOCK == 0 and past // MOBA_BLOCK <= LANES
    n_blocks = past // MOBA_BLOCK
    pages_per_step = min(16, n_pages)
    steps = n_pages // pages_per_step
    grid_spec = pltpu.PrefetchScalarGridSpec(
        num_scalar_prefetch=1, grid=(b, steps),
        in_specs=_page_specs(pages_per_step, layer, (PAGE_SIZE, 2 * width), lambda c: c * pages_per_step)
        + [pl.BlockSpec((None, 8, width), lambda i, c, pt: (i, 0, 0)),
           pl.BlockSpec((None, 1, 2 * width), lambda i, c, pt: (i, 0, 0)),
           pl.BlockSpec((8, pages_per_step * PAGE_SIZE), lambda i, c, pt: (0, c)),
           pl.BlockSpec((8, LANES), lambda i, c, pt: (0, 0))],
        out_specs=pl.BlockSpec((None, 1, width), lambda i, c, pt: (i, 0, 0)),
        scratch_shapes=[pltpu.VMEM((8, LANES), F32), pltpu.VMEM((8, LANES), F32), pltpu.VMEM((8, LANES), F32),
                        pltpu.VMEM((n_blocks, 8, width), F32)])
    return pl.pallas_call(
        functools.partial(_moba_sample_kernel, n_pages=pages_per_step, n_blocks=n_blocks),
        grid_spec=grid_spec,
        out_shape=jax.ShapeDtypeStruct((b, 1, width), BF16),
        compiler_params=_cparams("arbitrary", "arbitrary"),
    )(page_table, *([cache4] * pages_per_step), q_bd, kv_new, bias, bias0)


def _nsa_sample_kernel(pt_ref, *refs, n_pages, n_slc):
    pages = refs[:n_pages]
    (q_ref, g_ref, new_ref, win_ref, w16_ref, pos_ref, w1f_ref, b1_ref, w2_ref, b2_ref, cbias_ref, sbias_ref,
     wbias_ref, bias0_ref, ov_ref, o_ref, wout_ref, cmp_ref, slc_ref, s_ref) = refs[n_pages:]
    past = n_pages * PAGE_SIZE
    nsub = past // NSA_CMP_STRIDE
    hid = NSA_CMP_HIDDEN
    scale = 1.0 / math.sqrt(NSA_DK)
    nwin = win_ref.shape[0]
    for i in range(n_pages):
        page = pages[i][...]
        cmp_ref[i * PAGE_SIZE:(i + 1) * PAGE_SIZE, :] = page[:, :2 * NSA_DK]
        slc_ref[i * PAGE_SIZE:(i + 1) * PAGE_SIZE, :] = page[:, 2 * NSA_DK:].astype(BF16)

    q8 = q_ref[...].astype(BF16)
    q8f = q_ref[...]
    new = new_ref[...]
    head_row = lax.broadcasted_iota(jnp.int32, (8, 1), 0) < NSA_HEADS
    bias0 = bias0_ref[...][:, :1]

    hs = jnp.zeros((nsub, 4 * hid), F32)
    for j in range(NSA_CMP_STRIDE):
        r = cmp_ref[pl.ds(j, nsub, stride=NSA_CMP_STRIDE), :]
        hs = hs + _dot(r.astype(BF16), w16_ref[j])
    hs_new = _dot(jnp.broadcast_to(new[:, :2 * NSA_DK], (8, 2 * NSA_DK)).astype(BF16), w16_ref[0])[0:1, :]
    sub_i = lax.broadcasted_iota(jnp.int32, (nsub, hid), 0)
    outs = []
    for c in range(2):
        first = hs[:, 2 * c * hid:(2 * c + 1) * hid]
        second = pltpu.roll(hs[:, (2 * c + 1) * hid:(2 * c + 2) * hid], nsub - 1, 0)
        second = jnp.where(sub_i == nsub - 1, hs_new[:, (2 * c + 1) * hid:(2 * c + 2) * hid], second)
        h = first + second + _hpos(pos_ref, w1f_ref, c) + b1_ref[c]
        outs.append(_dot(jax.nn.silu(h).astype(BF16), w2_ref[c]) + b2_ref[c])
    kvc = jnp.concatenate(outs, axis=-1).astype(BF16)

    lc = _dot_nt(q8, kvc) * scale + cbias_ref[...]
    c_end = lax.broadcasted_iota(jnp.int32, (8, nsub), 1) * NSA_CMP_STRIDE + NSA_CMP_LEN - 1
    p_c = _masked_softmax(lc, c_end <= past)
    o_c = _dot(p_c.astype(BF16), kvc)
    p_sum = jnp.broadcast_to(jnp.sum(jnp.where(head_row, p_c, 0.0), axis=0, keepdims=True), (8, nsub))
    imp = jnp.dot(p_sum, ov_ref[...], preferred_element_type=F32, precision=lax.Precision.HIGHEST)[0:1, :]

    jj = lax.broadcasted_iota(jnp.int32, imp.shape, 1)
    cur = past // NSA_SLC_BLOCK
    s_ok = jj <= cur
    forced = (jj == 0) | (jj == cur) | (jj == cur - 1)
    score = jnp.where(forced, jnp.inf, jnp.where(s_ok, imp, NEG_INF))
    sel = ((_rank_before(score, n_slc) < min(NSA_TOPN, n_slc)) & s_ok).astype(F32)

    lane = lax.broadcasted_iota(jnp.int32, (8, PAGE_SIZE), 1)
    per_page = PAGE_SIZE // NSA_SLC_BLOCK
    for i in range(n_pages):
        kb = slc_ref[i * PAGE_SIZE:(i + 1) * PAGE_SIZE, :]
        s = _dot_nt(q8, kb) * scale + sbias_ref[:, i * PAGE_SIZE:(i + 1) * PAGE_SIZE]
        picked = jnp.zeros((8, PAGE_SIZE), F32)
        for u in range(per_page):
            picked = jnp.where(lane // NSA_SLC_BLOCK == u, sel[:, per_page * i + u:per_page * i + u + 1], picked)
        s_ref[:, i * PAGE_SIZE:(i + 1) * PAGE_SIZE] = jnp.where(picked > 0.5, s, NEG_INF)
    s_all = s_ref[...]
    s_self = jnp.sum(q8f * new[:, 2 * NSA_DK:4 * NSA_DK], axis=-1, keepdims=True) * scale + bias0
    m = jnp.maximum(jnp.max(s_all, axis=-1, keepdims=True), s_self)
    p = jnp.exp(s_all - m)
    e_self = jnp.exp(s_self - m)
    den = jnp.sum(p, axis=-1, keepdims=True) + e_self
    o_s = (_dot(p.astype(BF16), slc_ref[...])
           + e_self * new[:, 2 * NSA_DK:4 * NSA_DK]) / den

    win = win_ref[...]
    lw = _dot_nt(q8, win.astype(BF16)) * scale + wbias_ref[...]
    d_w = nwin - lax.broadcasted_iota(jnp.int32, (8, nwin), 1)
    lw = jnp.where(d_w < NSA_WINDOW, lw, NEG_INF)
    lw_self = jnp.sum(q8f * new[:, 4 * NSA_DK:6 * NSA_DK], axis=-1, keepdims=True) * scale + bias0
    m = jnp.maximum(jnp.max(lw, axis=-1, keepdims=True), lw_self)
    p = jnp.exp(lw - m)
    e_self = jnp.exp(lw_self - m)
    den = jnp.sum(p, axis=-1, keepdims=True) + e_self
    o_w = (_dot(p.astype(BF16), win.astype(BF16)) + e_self * new[:, 4 * NSA_DK:6 * NSA_DK]) / den

    g = jax.nn.sigmoid(g_ref[...])
    o_ref[...] = (g[:, 0:1] * o_c + g[:, 1:2] * o_s + g[:, 2:3] * o_w).astype(o_ref.dtype)
    wout_ref[0:nwin - 1, :] = win[1:nwin, :]
    wout_ref[nwin - 1:nwin, :] = new[:, 4 * NSA_DK:6 * NSA_DK]


def _nsa_sample(cache, layer, page_table, win_buf, q8, g8, new, w16, posf, w1f, b1, w2, b2,
                cbias, sbias, wbias, bias0, overlap, n_slc):
    b, n_pages = page_table.shape
    past = n_pages * PAGE_SIZE
    n_pool = cache.shape[1]
    cache4 = cache.reshape(cache.shape[0], n_pool, PAGE_SIZE, -1)
    width = cache4.shape[-1]
    nwin = win_buf.shape[2]
    win4 = win_buf.reshape(win_buf.shape[0], b, nwin, 2 * NSA_DK)
    full = lambda a: pl.BlockSpec(a.shape, lambda i, pt: (0,) * a.ndim)
    grid_spec = pltpu.PrefetchScalarGridSpec(
        num_scalar_prefetch=1, grid=(b,),
        in_specs=_page_specs(n_pages, layer, (PAGE_SIZE, width), lambda: 0)
        + [pl.BlockSpec((None, 8, LANES), lambda i, pt: (i, 0, 0)),
           pl.BlockSpec((None, 8, LANES), lambda i, pt: (i, 0, 0)),
           pl.BlockSpec((None, 1, new.shape[-1]), lambda i, pt: (i, 0, 0)),
           pl.BlockSpec((None, None, nwin, 2 * NSA_DK), lambda i, pt: (layer, i, 0, 0)),
           full(w16), full(posf), full(w1f), full(b1), full(w2), full(b2),
           full(cbias), full(sbias), full(wbias), full(bias0), full(overlap)],
        out_specs=[pl.BlockSpec((None, 8, LANES), lambda i, pt: (i, 0, 0)),
                   pl.BlockSpec((None, nwin, 2 * NSA_DK), lambda i, pt: (i, 0, 0))],
        scratch_shapes=[pltpu.VMEM((past, 2 * NSA_DK), F32), pltpu.VMEM((past, 2 * NSA_DK), BF16),
                        pltpu.VMEM((8, past), F32)])
    return pl.pallas_call(
        functools.partial(_nsa_sample_kernel, n_pages=n_pages, n_slc=n_slc),
        grid_spec=grid_spec,
        out_shape=[jax.ShapeDtypeStruct((b, 8, LANES), F32),
                   jax.ShapeDtypeStruct((b, nwin, 2 * NSA_DK), F32)],
        compiler_params=_cparams("arbitrary"),
    )(page_table, *([cache4] * n_pages), q8, g8, new, win4, w16, posf, w1f, b1, w2, b2,
      cbias, sbias, wbias, bias0, overlap)


def _rot_half_cols(w):
    half = MLA_ROPE // 2
    return jnp.concatenate([-w[..., half:], w[..., :half]], axis=-1)


def _pack_w_in(w):
    d = w.shape[0]
    sizes = (256, 256, 256, 256, 384, 12, 256, 128, 32, 3 * d)
    offs = np.concatenate([[0], np.cumsum(sizes)])
    mq, mk, mv, nq, nkv, ng, cq, ckv, kpe, mg = [w[:, offs[i]:offs[i + 1]] for i in range(10)]
    z = lambda n: jnp.zeros((d, n), w.dtype)
    out = jnp.concatenate([mq, mk, mv, nq, nkv, ng, z(C_CQ - C_NG - 12), cq, ckv, kpe, _rot_half_cols(kpe),
                           z(C_MG - C_KPER - MLA_ROPE), mg], axis=1)
    assert out.shape[1] == N_PAD
    return out.astype(BF16)


def _pack_w16(w1):
    st = NSA_CMP_STRIDE
    blocks = []
    for c in range(2):
        wc = jnp.concatenate([w1[c, :st], w1[c, st:]], axis=-1)
        zc = jnp.zeros_like(wc)
        blocks.append(jnp.concatenate([wc, zc] if c == 0 else [zc, wc], axis=-1))
    return jnp.concatenate(blocks, axis=1).astype(BF16)


def _rope_tables(pos, n_rep):
    half = MLA_ROPE // 2
    freq = ROPE_THETA ** (-jnp.arange(half, dtype=F32) / half)
    ang = pos.astype(F32)[:, None] * freq[None, :]
    cos = jnp.tile(jnp.cos(ang), (1, 2 * n_rep))
    sin = jnp.tile(jnp.sin(ang), (1, 2 * n_rep))
    return cos, sin


def _overlap_matrix(nc, n_slc, n_cols):
    ci = np.arange(nc)[:, None] * NSA_CMP_STRIDE
    sj = np.arange(n_cols)[None, :] * NSA_SLC_BLOCK
    ov = (ci < sj + NSA_SLC_BLOCK) & (ci + NSA_CMP_LEN > sj) & (np.arange(n_cols)[None, :] < n_slc)
    return jnp.asarray(ov.astype(np.float32))


def kernel(x_prompt, x_sample, cache_moba_kv, cache_nsa_kv, cache_mla, state_nsa_win, page_table,
           p_prompt, p_sample, t5_bias, norm_mix, w_in, mla_q_norm, mla_kv_norm, w_q_up, w_kv_up,
           nsa_cmp_w1, nsa_cmp_b1, nsa_cmp_w2, nsa_cmp_b2, nsa_cmp_pos, w_branch, w_out, norm_ffn,
           w_ffn_gate, w_ffn_up, w_ffn_down, norm_ple, w_ple_gate, w_ple_proj, norm_final):
    depth = w_in.shape[0]
    bsz, seq, d = x_prompt.shape
    dec_b, dec_s, _ = x_sample.shape
    n_pages = page_table.shape[1]
    past = n_pages * PAGE_SIZE
    nwin = state_nsa_win.shape[2]
    assert dec_s == 1 and nwin == NSA_WINDOW and d == C_MG // 2
    wp = min(NSA_WINDOW, seq)
    page_table = page_table.astype(jnp.int32)

    nt = seq // ATT_TILE
    n_delta = max(_n_distinct_deltas(ATT_TILE, nt), min(3, nt))
    tiles = t5_bias[jnp.asarray(_toeplitz_buckets(ATT_TILE, n_delta))]
    tiles = jnp.transpose(tiles, (0, 3, 1, 2))
    bias_moba_tiles = tiles[:, :MOBA_HEADS]
    bias_nsa_tiles = tiles[:, MOBA_HEADS:]
    nc_p = seq // NSA_CMP_STRIDE
    c_end_p = np.arange(nc_p) * NSA_CMP_STRIDE + NSA_CMP_LEN - 1
    cbias_p = jnp.transpose(t5_bias[jnp.asarray(_t5_bucket_np(np.arange(seq)[:, None] - c_end_p[None, :]))][..., MOBA_HEADS:],
                            (2, 0, 1))
    n_slc_p = -(-seq // NSA_SLC_BLOCK)
    overlap_p = _overlap_matrix(nc_p, n_slc_p, n_slc_p)

    def pad_rows(a):
        return jnp.concatenate([a, jnp.zeros_like(a)], axis=0)

    bias_by_dist = t5_bias[jnp.asarray(_t5_bucket_np(past - np.arange(past)))]
    sbias_moba = pad_rows(bias_by_dist[:, :MOBA_HEADS].T)
    sbias_nsa = pad_rows(bias_by_dist[:, MOBA_HEADS:].T)
    nsub_s = past // NSA_CMP_STRIDE
    c_end_s = np.arange(nsub_s) * NSA_CMP_STRIDE + NSA_CMP_LEN - 1
    cbias_s = pad_rows(t5_bias[jnp.asarray(_t5_bucket_np(past - c_end_s))][:, MOBA_HEADS:].T)
    wbias_s = pad_rows(t5_bias[jnp.asarray(_t5_bucket_np(nwin - np.arange(nwin)))][:, MOBA_HEADS:].T)
    bias0 = jnp.broadcast_to(t5_bias[int(_t5_bucket_np(np.array(0)))][:, None], (8, LANES))
    bias0_moba = jnp.concatenate([bias0[:MOBA_HEADS], jnp.zeros((8 - MOBA_HEADS, LANES), F32)], axis=0)
    bias0_nsa = jnp.concatenate([bias0[MOBA_HEADS:], jnp.zeros((8 - NSA_HEADS, LANES), F32)], axis=0)
    n_slc_s = -(-(past + dec_s) // NSA_SLC_BLOCK)
    slc_cols = -(-n_slc_s // LANES) * LANES
    overlap_s = _overlap_matrix(nsub_s, n_slc_s, slc_cols)

    cos_p, sin_p = _rope_tables(jnp.arange(seq, dtype=jnp.int32), MLA_HEADS)
    cos_s, sin_s = _rope_tables(jnp.full((dec_b,), past, jnp.int32), MLA_HEADS)

    xp = x_prompt.reshape(bsz * seq, d)
    xs = x_sample.reshape(dec_b, d)
    outs = [[] for _ in range(8)]
    for l in range(depth):
        w_in_p = _pack_w_in(w_in[l])
        wq = w_q_up[l].reshape(MLA_Q_RANK, MLA_HEADS, MLA_NOPE + MLA_ROPE)
        wq_nope = wq[:, :, :MLA_NOPE].reshape(MLA_Q_RANK, -1).astype(BF16)
        wq_pe = wq[:, :, MLA_NOPE:]
        wq_per = _rot_half_cols(wq_pe).reshape(MLA_Q_RANK, -1).astype(BF16)
        wq_pe = wq_pe.reshape(MLA_Q_RANK, -1).astype(BF16)
        wk_abs = jnp.transpose(w_kv_up[l][:, :, :MLA_NOPE], (1, 2, 0)).astype(BF16)
        wuv = jnp.transpose(w_kv_up[l][:, :, MLA_NOPE:], (1, 0, 2)).astype(BF16)
        g_mix = norm_mix[l][None, :]
        qn = mla_q_norm[l][None, :]
        kn = mla_kv_norm[l][None, :]
        w16 = _pack_w16(nsa_cmp_w1[l])
        posf = nsa_cmp_pos[l].reshape(2, 1, NSA_CMP_LEN * NSA_DK)
        w1f = nsa_cmp_w1[l].reshape(2, NSA_CMP_LEN * NSA_DK, NSA_CMP_HIDDEN)
        b1 = nsa_cmp_b1[l][:, None, :]
        w2 = nsa_cmp_w2[l].astype(BF16)
        b2 = nsa_cmp_b2[l][:, None, :]
        wb = w_branch[l].astype(BF16)
        wo = w_out[l].astype(BF16)
        ffn_w = (norm_ffn[l][None, :], w_ffn_gate[l].astype(BF16), w_ffn_up[l].astype(BF16),
                 w_ffn_down[l].astype(BF16), norm_ple[l][None, :], w_ple_gate[l].astype(BF16),
                 w_ple_proj[l].astype(BF16))
        last = l == depth - 1

        z = _inproj(xp, g_mix, w_in_p)
        z3 = z.reshape(bsz, seq, N_PAD)
        q_cat, mla_row, k_cat = _mla_prep(z, bsz, seq, qn, kn, wq_nope, wq_pe, wq_per, wk_abs, cos_p, sin_p)
        o_m = _moba_prompt(z, bsz, seq, bias_moba_tiles)
        kvc = _nsa_compress_prompt(z, bsz, seq, w16, posf, w1f, b1, w2, b2)
        o_n = _nsa_prompt(z, bsz, seq, kvc, cbias_p, bias_nsa_tiles, overlap_p)
        o_l = _mla_attn_prompt(q_cat, k_cat)
        xp3 = _mixer_out(xp.reshape(bsz, seq, d), z, o_m, o_n, o_l, wuv, wb, wo)
        xp = _ffn_ple(xp3.reshape(bsz * seq, d), *ffn_w, p_prompt[l].reshape(bsz * seq, -1),
                      norm_final[None, :], last)
        outs[0].append(z3[:, :, C_MK:C_MK + 512].reshape(bsz, seq, 2, MOBA_HEADS, HEAD_DIM))
        outs[2].append(z3[:, :, C_NKV:C_NKV + 4 * NSA_DK].reshape(bsz, seq, 4, NSA_DK))
        outs[4].append(mla_row)
        outs[6].append(z3[:, seq - wp:, C_NKV + 4 * NSA_DK:C_NKV + 6 * NSA_DK].reshape(bsz, wp, 2, NSA_DK))

        zs = _inproj(xs, g_mix, w_in_p)
        q_cat_s, mla_row_s, _ = _mla_prep(zs, 1, dec_b, qn, kn, wq_nope, wq_pe, wq_per, wk_abs, cos_s, sin_s)
        mq = zs[:, C_MQ:C_MQ + 256].reshape(dec_b, MOBA_HEADS, 1, HEAD_DIM)
        eye = jnp.eye(MOBA_HEADS, dtype=F32)[None, :, :, None]
        q_bd = (mq * eye).reshape(dec_b, MOBA_HEADS, MOBA_HEADS * HEAD_DIM)
        q_bd = jnp.concatenate([q_bd, jnp.zeros_like(q_bd)], axis=1)
        o_m = _moba_sample(cache_moba_kv, l, page_table, q_bd, zs[:, None, C_MK:C_MK + 512],
                           sbias_moba, bias0_moba)
        nq = zs[:, C_NQ:C_NQ + 256].reshape(dec_b, NSA_HEADS, NSA_DK)
        q8 = jnp.pad(nq, ((0, 0), (0, 8 - NSA_HEADS), (0, LANES - NSA_DK)))
        g8 = jnp.pad(zs[:, C_NG:C_NG + 3 * NSA_HEADS].reshape(dec_b, NSA_HEADS, 3),
                     ((0, 0), (0, 8 - NSA_HEADS), (0, LANES - 3)))
        o_n8, win_new = _nsa_sample(cache_nsa_kv, l, page_table, state_nsa_win, q8, g8,
                                    zs[:, None, C_NKV:C_NKV + 6 * NSA_DK], w16, posf, w1f, b1, w2, b2,
                                    cbias_s, sbias_nsa, wbias_s, bias0_nsa, overlap_s, n_slc_s)
        o_n = o_n8[:, :NSA_HEADS, NSA_DK:].reshape(dec_b, NSA_HEADS * NSA_DK).astype(BF16)
        o_l = _mla_sample(cache_mla, l, page_table, jnp.transpose(q_cat_s[0], (1, 0, 2)),
                          mla_row_s.reshape(dec_b, 1, -1))
        xs3 = _mixer_out(xs[None], zs, o_m.reshape(1, dec_b, -1), o_n[None],
                         jnp.transpose(o_l, (1, 0, 2))[None], wuv, wb, wo)
        xs = _ffn_ple(xs3[0], *ffn_w, p_sample[l].reshape(dec_b, -1), norm_final[None, :], last)
        outs[1].append(zs[:, C_MK:C_MK + 512].reshape(dec_b, 1, 2, MOBA_HEADS, HEAD_DIM))
        outs[3].append(zs[:, C_NKV:C_NKV + 4 * NSA_DK].reshape(dec_b, 1, 4, NSA_DK))
        outs[5].append(mla_row_s.reshape(dec_b, 1, -1))
        outs[7].append(win_new.reshape(dec_b, nwin, 2, NSA_DK))

    return (xp.reshape(bsz, seq, d), xs.reshape(dec_b, dec_s, d)) + tuple(jnp.stack(o) for o in outs)
```

```python
import functools
import math

import numpy as np
import jax
import jax.numpy as jnp
from jax import lax
from jax.experimental import pallas as pl
from jax.experimental.pallas import tpu as pltpu

PAGE_SIZE = 128
HEAD_DIM = 64
MOBA_HEADS = 4
MOBA_BLOCK = 256
MOBA_TOPK = 3
NSA_HEADS = 4
NSA_DK = 64
NSA_CMP_LEN = 32
NSA_CMP_STRIDE = 16
NSA_CMP_HIDDEN = 128
NSA_SLC_BLOCK = 64
NSA_TOPN = 16
NSA_WINDOW = 512
MLA_HEADS = 8
MLA_Q_RANK = 256
MLA_KV_RANK = 128
MLA_NOPE = 64
MLA_ROPE = 32
MLA_V = 64
ROPE_THETA = 10000.0
T5_BUCKETS = 32
T5_MAX_DIST = 128
EPS = 1e-6

LANES = 128
ATT_TILE = 256
KEY_STEP = 512
VMEM_LIMIT = 56 * 1024 * 1024

C_MQ, C_MK, C_MV, C_NQ, C_NKV, C_NG = 0, 256, 512, 768, 1024, 1408
C_CQ, C_CKV, C_KPE, C_KPER, C_MG, N_PAD = 1536, 1792, 1920, 1952, 2048, 5120

BF16 = jnp.bfloat16
F32 = jnp.float32
NEG_INF = float("-inf")
MLA_SCALE = 1.0 / math.sqrt(MLA_NOPE + MLA_ROPE)


def _cparams(*sem):
    return pltpu.CompilerParams(dimension_semantics=sem, vmem_limit_bytes=VMEM_LIMIT)


def _rms(x, g):
    return x * lax.rsqrt(jnp.mean(x * x, axis=-1, keepdims=True) + EPS) * g


def _dot(a, b, precision=None):
    return jnp.dot(a, b, preferred_element_type=F32, precision=precision)


def _dot_nt(a, b, precision=None):
    return lax.dot_general(a, b, (((1,), (1,)), ((), ())), preferred_element_type=F32, precision=precision)


def _masked_softmax(s, mask):
    s = jnp.where(mask, s, NEG_INF)
    m = jnp.max(s, axis=-1, keepdims=True)
    m = jnp.where(m == NEG_INF, 0.0, m)
    e = jnp.exp(s - m)
    return e / jnp.maximum(jnp.sum(e, axis=-1, keepdims=True), 1e-30)


def _exp_rows(s):
    e = jnp.exp(s - jnp.max(s, axis=-1, keepdims=True))
    return e, jnp.sum(e, axis=-1, keepdims=True)


def _rank_before(score, n):
    lane = lax.broadcasted_iota(jnp.int32, score.shape, score.ndim - 1)
    rank = jnp.zeros(score.shape, jnp.int32)
    for m in range(n):
        sm = score[..., m:m + 1]
        before = (sm > score) | ((sm == score) & (m < lane))
        rank = rank + before.astype(jnp.int32)
    return rank


def _t5_bucket_np(dist):
    n = np.maximum(dist, 0)
    max_exact = T5_BUCKETS // 2
    nf = np.maximum(n, 1).astype(np.float32)
    large = max_exact + (np.log(nf / max_exact) / math.log(T5_MAX_DIST / max_exact)
                         * (T5_BUCKETS - max_exact)).astype(np.int32)
    return np.where(n < max_exact, n, np.minimum(large, T5_BUCKETS - 1)).astype(np.int32)


def _toeplitz_buckets(tile, n_delta):
    i = np.arange(tile)[:, None]
    j = np.arange(tile)[None, :]
    return np.stack([_t5_bucket_np(d * tile + i - j) for d in range(n_delta)])


def _n_distinct_deltas(tile, n_tiles):
    for n in range(1, n_tiles + 1):
        lo = (n - 1) * tile - (tile - 1)
        if lo >= 0 and np.all(_t5_bucket_np(np.arange(lo, n_tiles * tile)) == _t5_bucket_np(np.array(lo))):
            return n
    return n_tiles


def _lookup(table, idx, head_axis):
    idx = jnp.expand_dims(jnp.asarray(idx), head_axis)
    shape = [1] * idx.ndim
    shape[head_axis] = table.shape[1]
    out = jnp.zeros(np.broadcast_shapes(idx.shape, tuple(shape)), F32)
    for b in range(table.shape[0]):
        out = jnp.where(idx == b, table[b].reshape(shape), out)
    return out


def _block_indicator(rows, n_blocks, block, width):
    j = np.arange(rows)[:, None]
    k = np.arange(width)[None, :]
    return jnp.asarray(((k // block == j) & (j < n_blocks)).astype(np.float32), dtype=BF16)


def _inproj_kernel(x_ref, g_ref, w_ref, o_ref, mg_ref):
    xn = _rms(x_ref[...], g_ref[...]).astype(BF16)
    z = _dot(xn, w_ref[...])
    o_ref[...] = z[:, :C_MG]
    mg_ref[...] = z[:, C_MG:].astype(BF16)


def _inproj(x, g, w):
    m = x.shape[0]
    tm = min(256, m)
    n_gate = w.shape[1] - C_MG
    return pl.pallas_call(
        _inproj_kernel,
        grid=(m // tm,),
        in_specs=[pl.BlockSpec((tm, x.shape[1]), lambda i: (i, 0)),
                  pl.BlockSpec((1, x.shape[1]), lambda i: (0, 0)),
                  pl.BlockSpec(w.shape, lambda i: (0, 0))],
        out_specs=[pl.BlockSpec((tm, C_MG), lambda i: (i, 0)), pl.BlockSpec((tm, n_gate), lambda i: (i, 0))],
        out_shape=[jax.ShapeDtypeStruct((m, C_MG), F32), jax.ShapeDtypeStruct((m, n_gate), BF16)],
        compiler_params=_cparams("parallel"),
    )(x, g, w)


def _mla_prep_kernel(cq_ref, ck_ref, qn_ref, kn_ref, wn_ref, wp_ref, wpr_ref, wk_ref, cos_ref, sin_ref,
                     qcat_ref, row_ref, kcat_ref):
    tm = cq_ref.shape[0]
    cqn = _rms(cq_ref[...], qn_ref[...]).astype(BF16)
    q_nope = _dot(cqn, wn_ref[...])
    cos = cos_ref[...]
    sin = sin_ref[...]
    q_pe = _dot(cqn, wp_ref[...]) * cos + _dot(cqn, wpr_ref[...]) * sin
    ck = ck_ref[...]
    ckv_n = _rms(ck[:, :MLA_KV_RANK], kn_ref[...])
    k_pe = (ck[:, MLA_KV_RANK:MLA_KV_RANK + MLA_ROPE] * cos[:, :MLA_ROPE]
            + ck[:, MLA_KV_RANK + MLA_ROPE:MLA_KV_RANK + 2 * MLA_ROPE] * sin[:, :MLA_ROPE])
    row_ref[...] = jnp.concatenate([ckv_n, k_pe], axis=-1)
    pad = jnp.zeros((tm, 2 * LANES - MLA_KV_RANK - MLA_ROPE), F32)
    kcat_ref[...] = jnp.concatenate([ckv_n, k_pe, pad], axis=-1).astype(BF16)
    for h in range(MLA_HEADS):
        q_lat = _dot(q_nope[:, h * MLA_NOPE:(h + 1) * MLA_NOPE].astype(BF16), wk_ref[h])
        q_h = jnp.concatenate([q_lat, q_pe[:, h * MLA_ROPE:(h + 1) * MLA_ROPE], pad], axis=-1)
        qcat_ref[h] = (q_h * MLA_SCALE).astype(BF16)


def _mla_prep(z, bv, sv, qn, kn, wn, wp, wpr, wk, cos, sin):
    z3 = z.reshape(bv, sv, -1)
    tm = min(256, sv)
    full = lambda a: pl.BlockSpec(a.shape, lambda b, i: (0,) * a.ndim)
    return pl.pallas_call(
        _mla_prep_kernel,
        grid=(bv, sv // tm),
        in_specs=[pl.BlockSpec((None, tm, 256), lambda b, i: (b, i, C_CQ // 256)),
                  pl.BlockSpec((None, tm, 256), lambda b, i: (b, i, C_CKV // 256)),
                  full(qn), full(kn), full(wn), full(wp), full(wpr), full(wk),
                  pl.BlockSpec((tm, 256), lambda b, i: (i, 0)),
                  pl.BlockSpec((tm, 256), lambda b, i: (i, 0))],
        out_specs=[pl.BlockSpec((None, MLA_HEADS, tm, 256), lambda b, i: (b, 0, i, 0)),
                   pl.BlockSpec((None, tm, MLA_KV_RANK + MLA_ROPE), lambda b, i: (b, i, 0)),
                   pl.BlockSpec((None, tm, 256), lambda b, i: (b, i, 0))],
        out_shape=[jax.ShapeDtypeStruct((bv, MLA_HEADS, sv, 256), BF16),
                   jax.ShapeDtypeStruct((bv, sv, MLA_KV_RANK + MLA_ROPE), F32),
                   jax.ShapeDtypeStruct((bv, sv, 256), BF16)],
        compiler_params=_cparams("parallel", "parallel"),
    )(z3, z3, qn, kn, wn, wp, wpr, wk, cos, sin)


def _mixer_out_kernel(om_ref, on_ref, ol_ref, g0_ref, g1_ref, g2_ref, x_ref, wuv_ref, wb_ref, wo_ref, o_ref):
    o_mla = jnp.concatenate([_dot(ol_ref[h], wuv_ref[h]) for h in range(MLA_HEADS)], axis=-1).astype(BF16)
    n_a = om_ref.shape[-1]
    n_b = n_a + on_ref.shape[-1]
    b_moba = _dot(om_ref[...], wb_ref[:n_a, :])
    b_nsa = _dot(on_ref[...], wb_ref[n_a:n_b, :])
    b_mla = _dot(o_mla, wb_ref[n_b:, :])
    h = (jax.nn.sigmoid(g0_ref[...].astype(F32)) * b_moba + jax.nn.sigmoid(g1_ref[...].astype(F32)) * b_nsa
         + jax.nn.sigmoid(g2_ref[...].astype(F32)) * b_mla)
    o_ref[...] = x_ref[...] + _dot(h.astype(BF16), wo_ref[...])


def _mixer_out(x, mg, o_moba, o_nsa, o_lat, wuv, wb, wo):
    bv, sv, d = x.shape
    z3 = mg.reshape(bv, sv, -1)
    tm = min(256, sv)
    full = lambda a: pl.BlockSpec(a.shape, lambda b, i: (0,) * a.ndim)
    gate = lambda k: pl.BlockSpec((None, tm, d), lambda b, i: (b, i, k))
    row = lambda w: pl.BlockSpec((None, tm, w), lambda b, i: (b, i, 0))
    return pl.pallas_call(
        _mixer_out_kernel,
        grid=(bv, sv // tm),
        in_specs=[row(o_moba.shape[-1]), row(o_nsa.shape[-1]),
                  pl.BlockSpec((None, MLA_HEADS, tm, MLA_KV_RANK), lambda b, i: (b, 0, i, 0)),
                  gate(0), gate(1), gate(2), row(d), full(wuv), full(wb), full(wo)],
        out_specs=row(d),
        out_shape=jax.ShapeDtypeStruct(x.shape, F32),
        compiler_params=_cparams("parallel", "parallel"),
    )(o_moba, o_nsa, o_lat, z3, z3, z3, x, wuv, wb, wo)


def _ffn_ple_kernel(x_ref, gf_ref, wg_ref, wu_ref, wd_ref, gp_ref, wpg_ref, wpp_ref, p_ref, gl_ref,
                    o_ref, xn_ref, acc_ref, *, final_norm):
    f = pl.program_id(1)

    @pl.when(f == 0)
    def _():
        xn_ref[...] = _rms(x_ref[...], gf_ref[...]).astype(BF16)
        acc_ref[...] = jnp.zeros_like(acc_ref)

    xn = xn_ref[...]
    a = jax.nn.silu(_dot(xn, wg_ref[...])) * _dot(xn, wu_ref[...])
    acc_ref[...] += _dot(a.astype(BF16), wd_ref[...])

    @pl.when(f == pl.num_programs(1) - 1)
    def _():
        x2 = x_ref[...] + acc_ref[...]
        gate = jax.nn.sigmoid(_dot(_rms(x2, gp_ref[...]).astype(BF16), wpg_ref[...]))
        x3 = x2 + gate * _dot(p_ref[...].astype(BF16), wpp_ref[...])
        o_ref[...] = _rms(x3, gl_ref[...]) if final_norm else x3


def _ffn_ple(x, gf, wg, wu, wd, gp, wpg, wpp, p, gl, final_norm):
    m, d = x.shape
    dff = wg.shape[1]
    tm = min(512, m)
    tf = dff // 2 if (dff // 2) % LANES == 0 else dff
    full = lambda a: pl.BlockSpec(a.shape, lambda i, f: (0,) * a.ndim)
    return pl.pallas_call(
        functools.partial(_ffn_ple_kernel, final_norm=final_norm),
        grid=(m // tm, dff // tf),
        in_specs=[pl.BlockSpec((tm, d), lambda i, f: (i, 0)), full(gf),
                  pl.BlockSpec((d, tf), lambda i, f: (0, f)),
                  pl.BlockSpec((d, tf), lambda i, f: (0, f)),
                  pl.BlockSpec((tf, d), lambda i, f: (f, 0)),
                  full(gp), full(wpg), full(wpp),
                  pl.BlockSpec((tm, p.shape[1]), lambda i, f: (i, 0)), full(gl)],
        out_specs=pl.BlockSpec((tm, d), lambda i, f: (i, 0)),
        out_shape=jax.ShapeDtypeStruct((m, d), F32),
        scratch_shapes=[pltpu.VMEM((tm, d), BF16), pltpu.VMEM((tm, d), F32)],
        compiler_params=_cparams("parallel", "arbitrary"),
    )(x, gf, wg, wu, wd, gp, wpg, wpp, p, gl)


def _for_prefix_width(q_end, step, total, body):
    for v in range(total // step):
        pl.when((q_end - 1) // step == v)(functools.partial(body, (v + 1) * step))


def _mla_attn_kernel(q_ref, k_ref, o_ref, *, tq, step):
    qi = pl.program_id(1)
    rows = MLA_HEADS * tq
    q = q_ref[...].reshape(rows, q_ref.shape[-1])
    t_pos = qi * tq + lax.broadcasted_iota(jnp.int32, (MLA_HEADS, tq, 1), 1).reshape(rows, 1)

    def body(width):
        k = k_ref[0:width, :]
        s = _dot_nt(q, k)
        k_pos = lax.broadcasted_iota(jnp.int32, (rows, width), 1)
        e, den = _exp_rows(jnp.where(k_pos <= t_pos, s, NEG_INF))
        o = _dot(e.astype(BF16), k[:, :MLA_KV_RANK]) / den
        o_ref[...] = o.reshape(MLA_HEADS, tq, MLA_KV_RANK).astype(o_ref.dtype)

    _for_prefix_width((qi + 1) * tq, step, k_ref.shape[0], body)


def _mla_attn_prompt(q_cat, k_cat):
    b, nh, s, w = q_cat.shape
    tq = min(128, s)
    step = min(KEY_STEP, s)
    return pl.pallas_call(
        functools.partial(_mla_attn_kernel, tq=tq, step=step),
        grid=(b, s // tq),
        in_specs=[pl.BlockSpec((None, nh, tq, w), lambda b, i: (b, 0, i, 0)),
                  pl.BlockSpec((None, s, w), lambda b, i: (b, 0, 0))],
        out_specs=pl.BlockSpec((None, nh, tq, MLA_KV_RANK), lambda b, i: (b, 0, i, 0)),
        out_shape=jax.ShapeDtypeStruct((b, nh, s, MLA_KV_RANK), BF16),
        compiler_params=_cparams("parallel", "arbitrary"),
    )(q_cat, k_cat)


def _bias_row(bias_ref, h, qi, first_tile, n_tiles, n_delta):
    return jnp.concatenate([bias_ref[jnp.clip(qi - first_tile - n, 0, n_delta - 1), h] for n in range(n_tiles)],
                           axis=-1)


def _moba_prompt_kernel(q_ref, k_ref, v_ref, bias_ref, ind_ref, o_ref, kmean_ref, *, nb, n_delta, step):
    t = ATT_TILE
    qi = pl.program_id(1)
    width_all = MOBA_HEADS * HEAD_DIM
    pair = 2 * HEAD_DIM

    @pl.when(qi == 0)
    def _():
        kmean_ref[...] = jnp.zeros_like(kmean_ref)
        for n in range(nb):
            kmean_ref[n:n + 1, :] = jnp.mean(k_ref[n * t:(n + 1) * t, :], axis=0, keepdims=True)

    q = q_ref[...]
    lane = lax.broadcasted_iota(jnp.int32, (t, width_all), 1)
    blk = lax.broadcasted_iota(jnp.int32, (t, LANES), 1)
    lane2 = lax.broadcasted_iota(jnp.int32, (t, pair), 1)
    kmean = kmean_ref[...]
    sels, q2s = [], []
    for h in range(MOBA_HEADS):
        q_h = jnp.where((lane >= h * HEAD_DIM) & (lane < (h + 1) * HEAD_DIM), q, 0.0)
        gate = _dot_nt(q_h, kmean, precision=lax.Precision.HIGHEST)
        score = jnp.where(blk < qi, gate, NEG_INF)
        sel = (_rank_before(score, nb) < MOBA_TOPK) & (blk < qi)
        sels.append(sel.astype(BF16))
        hp, half = h // 2, h % 2
        q2 = q[:, hp * pair:(hp + 1) * pair]
        q2s.append(jnp.where((lane2 >= half * HEAD_DIM) & (lane2 < (half + 1) * HEAD_DIM), q2, 0.0).astype(BF16))

    def body(width):
        q_pos = qi * t + lax.broadcasted_iota(jnp.int32, (t, width), 0)
        k_pos = lax.broadcasted_iota(jnp.int32, (t, width), 1)
        own = (k_pos >= qi * t) & (k_pos <= q_pos)
        outs = []
        for h in range(MOBA_HEADS):
            hp, half = h // 2, h % 2
            kb = k_ref[0:width, hp * pair:(hp + 1) * pair].astype(BF16)
            vb = v_ref[0:width, hp * pair:(hp + 1) * pair].astype(BF16)
            picked = _dot(sels[h], ind_ref[:, 0:width])
            mask = jnp.where(own, 1.0, picked) > 0.5
            s = (_dot_nt(q2s[h], kb) * (1.0 / math.sqrt(HEAD_DIM))
                 + _bias_row(bias_ref, h, qi, 0, width // t, n_delta))
            e, den = _exp_rows(jnp.where(mask, s, NEG_INF))
            o2 = _dot(e.astype(BF16), vb) / den
            outs.append(o2[:, half * HEAD_DIM:(half + 1) * HEAD_DIM])
        o_ref[...] = jnp.concatenate(outs, axis=-1).astype(o_ref.dtype)

    _for_prefix_width((qi + 1) * t, step, k_ref.shape[0], body)


def _moba_prompt(z, b, s, bias_tiles):
    assert s % ATT_TILE == 0 and MOBA_BLOCK == ATT_TILE and s // MOBA_BLOCK <= LANES
    z3 = z.reshape(b, s, -1)
    nb = s // MOBA_BLOCK
    width = MOBA_HEADS * HEAD_DIM
    ind = _block_indicator(LANES, nb, MOBA_BLOCK, s)
    full = lambda a: pl.BlockSpec(a.shape, lambda b, i: (0,) * a.ndim)
    return pl.pallas_call(
        functools.partial(_moba_prompt_kernel, nb=nb, n_delta=bias_tiles.shape[0], step=min(KEY_STEP, s)),
        grid=(b, nb),
        in_specs=[pl.BlockSpec((None, ATT_TILE, width), lambda b, i: (b, i, C_MQ // width)),
                  pl.BlockSpec((None, s, width), lambda b, i: (b, 0, C_MK // width)),
                  pl.BlockSpec((None, s, width), lambda b, i: (b, 0, C_MV // width)),
                  full(bias_tiles), full(ind)],
        out_specs=pl.BlockSpec((None, ATT_TILE, width), lambda b, i: (b, i, 0)),
        out_shape=jax.ShapeDtypeStruct((b, s, width), BF16),
        scratch_shapes=[pltpu.VMEM((LANES, width), F32)],
        compiler_params=_cparams("parallel", "arbitrary"),
    )(z3, z3, z3, bias_tiles, ind)


def _hpos(pos_ref, w1f_ref, c):
    pos = jnp.broadcast_to(pos_ref[c], (8, pos_ref.shape[-1]))
    return _dot(pos, w1f_ref[c])[0:1, :]


def _nsa_compress_prompt_kernel(z_ref, w16_ref, pos_ref, w1f_ref, b1_ref, w2_ref, b2_ref, o_ref, *, nsub):
    hid = NSA_CMP_HIDDEN
    hs = jnp.zeros((nsub, 4 * hid), F32)
    for j in range(NSA_CMP_STRIDE):
        r = z_ref[pl.ds(j, nsub, stride=NSA_CMP_STRIDE), :]
        hs = hs + _dot(r.astype(BF16), w16_ref[j])
    outs = []
    for c in range(2):
        first = hs[:, 2 * c * hid:(2 * c + 1) * hid]
        second = pltpu.roll(hs[:, (2 * c + 1) * hid:(2 * c + 2) * hid], nsub - 1, 0)
        h = first + second + _hpos(pos_ref, w1f_ref, c) + b1_ref[c]
        outs.append(_dot(jax.nn.silu(h).astype(BF16), w2_ref[c]) + b2_ref[c])
    o_ref[...] = jnp.concatenate(outs, axis=-1)


def _nsa_compress_prompt(z, b, s, w16, posf, w1f, b1, w2, b2):
    z3 = z.reshape(b, s, -1)
    nsub = s // NSA_CMP_STRIDE
    full = lambda a: pl.BlockSpec(a.shape, lambda i: (0,) * a.ndim)
    return pl.pallas_call(
        functools.partial(_nsa_compress_prompt_kernel, nsub=nsub),
        grid=(b,),
        in_specs=[pl.BlockSpec((None, s, LANES), lambda i: (i, 0, C_NKV // LANES)),
                  full(w16), full(posf), full(w1f), full(b1), full(w2), full(b2)],
        out_specs=pl.BlockSpec((None, nsub, 2 * NSA_DK), lambda i: (i, 0, 0)),
        out_shape=jax.ShapeDtypeStruct((b, nsub, 2 * NSA_DK), F32),
        compiler_params=_cparams("parallel"),
    )(z3, w16, posf, w1f, b1, w2, b2)


def _nsa_prompt_kernel(q_ref, g_ref, kvc_ref, slc_ref, win_ref, cbias_ref, bias_ref, ov_ref, ind_ref, o_ref,
                       *, n_delta, n_slc, step, win_keys):
    t = ATT_TILE
    nh = NSA_HEADS
    qi = pl.program_id(1)
    scale = 1.0 / math.sqrt(NSA_DK)
    q = q_ref[...]
    zero = jnp.zeros((t, NSA_DK), F32)
    qs = [jnp.concatenate([q[:, h * NSA_DK:(h + 1) * NSA_DK], zero], axis=-1).astype(BF16) for h in range(nh)]
    t_pos = qi * t + lax.broadcasted_iota(jnp.int32, (t, 1), 0)

    kvc = kvc_ref[...].astype(BF16)
    nc = kvc.shape[0]
    c_ok = lax.broadcasted_iota(jnp.int32, (t, nc), 1) * NSA_CMP_STRIDE + NSA_CMP_LEN - 1 <= t_pos
    o_c = []
    p_sum = jnp.zeros((t, nc), F32)
    for h in range(nh):
        p_c = _masked_softmax(_dot_nt(qs[h], kvc) * scale + cbias_ref[h], c_ok)
        o_c.append(_dot(p_c.astype(BF16), kvc))
        p_sum = p_sum + p_c
    imp = _dot(p_sum, ov_ref[...], precision=lax.Precision.HIGHEST)

    jj = lax.broadcasted_iota(jnp.int32, (t, LANES), 1)
    cur = t_pos // NSA_SLC_BLOCK
    s_ok = jj <= cur
    forced = (jj == 0) | (jj == cur) | (jj == cur - 1)
    score = jnp.where(forced, jnp.inf, jnp.where(s_ok, imp, NEG_INF))
    sel = ((_rank_before(score, n_slc) < min(NSA_TOPN, n_slc)) & s_ok).astype(BF16)

    start = pl.multiple_of(jnp.maximum((qi + 1) * t - win_keys, 0), t)
    wb = win_ref[pl.ds(start, win_keys), :].astype(BF16)
    d_w = t_pos - (start + lax.broadcasted_iota(jnp.int32, (t, win_keys), 1))
    w_ok = (d_w >= 0) & (d_w < NSA_WINDOW)
    o_w = []
    for h in range(nh):
        s = _dot_nt(qs[h], wb) * scale + _bias_row(bias_ref, h, qi, start // t, win_keys // t, n_delta)
        e, den = _exp_rows(jnp.where(w_ok, s, NEG_INF))
        o_w.append(_dot(e.astype(BF16), wb) / den)

    g = jax.nn.sigmoid(g_ref[...])

    def body(width):
        kb = slc_ref[0:width, :].astype(BF16)
        picked = _dot(sel, ind_ref[:, 0:width])
        k_pos = lax.broadcasted_iota(jnp.int32, (t, width), 1)
        mask = jnp.where(k_pos <= t_pos, picked, 0.0) > 0.5
        outs = []
        for h in range(nh):
            s = _dot_nt(qs[h], kb) * scale + _bias_row(bias_ref, h, qi, 0, width // t, n_delta)
            e, den = _exp_rows(jnp.where(mask, s, NEG_INF))
            o_s = _dot(e.astype(BF16), kb) / den
            o_h = (g[:, 3 * h:3 * h + 1] * o_c[h] + g[:, 3 * h + 1:3 * h + 2] * o_s
                   + g[:, 3 * h + 2:3 * h + 3] * o_w[h])
            outs.append(o_h[:, NSA_DK:])
        o_ref[...] = jnp.concatenate(outs, axis=-1).astype(o_ref.dtype)

    _for_prefix_width((qi + 1) * t, step, slc_ref.shape[0], body)


def _nsa_prompt(z, b, s, kvc, cbias, bias_tiles, overlap):
    assert s % ATT_TILE == 0
    z3 = z.reshape(b, s, -1)
    nt = s // ATT_TILE
    nc = kvc.shape[1]
    n_slc = -(-s // NSA_SLC_BLOCK)
    assert n_slc <= LANES
    width = NSA_HEADS * NSA_DK
    win_keys = min(NSA_WINDOW + ATT_TILE, s)
    ind = _block_indicator(LANES, n_slc, NSA_SLC_BLOCK, s)
    full = lambda a: pl.BlockSpec(a.shape, lambda b, i: (0,) * a.ndim)
    seq = lambda col: pl.BlockSpec((None, s, LANES), lambda b, i: (b, 0, col // LANES))
    return pl.pallas_call(
        functools.partial(_nsa_prompt_kernel, n_delta=bias_tiles.shape[0], n_slc=n_slc, step=min(KEY_STEP, s),
                          win_keys=win_keys),
        grid=(b, nt),
        in_specs=[pl.BlockSpec((None, ATT_TILE, width), lambda b, i: (b, i, C_NQ // width)),
                  pl.BlockSpec((None, ATT_TILE, LANES), lambda b, i: (b, i, C_NG // LANES)),
                  pl.BlockSpec((None, nc, 2 * NSA_DK), lambda b, i: (b, 0, 0)),
                  seq(C_NKV + 2 * NSA_DK), seq(C_NKV + 4 * NSA_DK),
                  pl.BlockSpec((NSA_HEADS, ATT_TILE, nc), lambda b, i: (0, i, 0)),
                  full(bias_tiles), full(overlap), full(ind)],
        out_specs=pl.BlockSpec((None, ATT_TILE, width), lambda b, i: (b, i, 0)),
        out_shape=jax.ShapeDtypeStruct((b, s, width), BF16),
        compiler_params=_cparams("parallel", "parallel"),
    )(z3, z3, kvc, z3, z3, cbias, bias_tiles, overlap, ind)


def _page_specs(n, layer, shape, first_page):
    def spec(i):
        return pl.BlockSpec((None, None) + shape, lambda b, *a: (layer, a[-1][b, first_page(*a[:-1]) + i], 0, 0))
    return [spec(i) for i in range(n)]


def _mla_sample_kernel(pt_ref, *refs, n_pages):
    pages = refs[:n_pages]
    q_ref, knew_ref, o_ref, kbuf_ref = refs[n_pages:]
    width = MLA_KV_RANK + MLA_ROPE
    past = n_pages * PAGE_SIZE

    @pl.when(pl.program_id(0) == 0)
    def _():
        kbuf_ref[width:, :] = jnp.zeros((kbuf_ref.shape[0] - width, past), BF16)

    for i in range(n_pages):
        kbuf_ref[0:width, i * PAGE_SIZE:(i + 1) * PAGE_SIZE] = pages[i][...].astype(BF16)
    q = q_ref[...]
    k_new = knew_ref[...]
    kbuf = kbuf_ref[...]
    s = _dot(q, kbuf)
    s_new = jnp.sum(q[:, :width].astype(F32) * k_new, axis=-1, keepdims=True)
    m = jnp.maximum(jnp.max(s, axis=-1, keepdims=True), s_new)
    p = jnp.exp(s - m)
    p_new = jnp.exp(s_new - m)
    den = jnp.sum(p, axis=-1, keepdims=True) + p_new
    o = _dot_nt(p.astype(BF16), kbuf[:MLA_KV_RANK, :]) + p_new * k_new[:, :MLA_KV_RANK]
    o_ref[...] = (o / den).astype(o_ref.dtype)


def _mla_sample(cache_t, layer, page_table, q_cat, row_new):
    b, n_pages = page_table.shape
    width = cache_t.shape[2]
    grid_spec = pltpu.PrefetchScalarGridSpec(
        num_scalar_prefetch=1, grid=(b,),
        in_specs=_page_specs(n_pages, layer, (width, PAGE_SIZE), lambda: 0)
        + [pl.BlockSpec((None, MLA_HEADS, q_cat.shape[-1]), lambda i, pt: (i, 0, 0)),
           pl.BlockSpec((None, 1, width), lambda i, pt: (i, 0, 0))],
        out_specs=pl.BlockSpec((None, MLA_HEADS, MLA_KV_RANK), lambda i, pt: (i, 0, 0)),
        scratch_shapes=[pltpu.VMEM((q_cat.shape[-1], n_pages * PAGE_SIZE), BF16)])
    return pl.pallas_call(
        functools.partial(_mla_sample_kernel, n_pages=n_pages),
        grid_spec=grid_spec,
        out_shape=jax.ShapeDtypeStruct((b, MLA_HEADS, MLA_KV_RANK), BF16),
        compiler_params=_cparams("arbitrary"),
    )(page_table, *([cache_t] * n_pages), q_cat, row_new)


def _moba_sample_kernel(pt_ref, *refs, n_pages, n_blocks):
    pages = refs[:n_pages]
    q_ref, qcol_ref, kvnew_ref, bias_ref, bias0_ref, o_ref, m_ref, l_ref, g_ref, acc_ref = refs[n_pages:]
    c = pl.program_id(1)
    width = MOBA_HEADS * HEAD_DIM
    per_blk = MOBA_BLOCK // PAGE_SIZE
    blk_per_step = n_pages // per_blk
    keys = n_pages * PAGE_SIZE
    q = q_ref[...]
    qb = q.astype(BF16)
    qcol = qcol_ref[...]
    lane = lax.broadcasted_iota(jnp.int32, (8, LANES), 1)
    row = lax.broadcasted_iota(jnp.int32, (8, LANES), 0)

    @pl.when(c == 0)
    def _():
        m_ref[...] = jnp.full_like(m_ref, NEG_INF)
        l_ref[...] = jnp.zeros_like(l_ref)
        g_ref[...] = jnp.full_like(g_ref, NEG_INF)

    s = jnp.concatenate([_dot(qb, pages[u][0:width, :].astype(BF16)) for u in range(n_pages)], axis=-1)
    s = s * (1.0 / math.sqrt(HEAD_DIM)) + bias_ref[...]
    key_blk = lax.broadcasted_iota(jnp.int32, (8, keys), 1) // MOBA_BLOCK
    m_key = jnp.zeros((8, keys), F32)
    m_blk = []
    for j in range(blk_per_step):
        m_j = jnp.max(jnp.where(key_blk == j, s, NEG_INF), axis=-1, keepdims=True)
        m_blk.append(m_j)
        m_key = jnp.where(key_blk == j, m_j, m_key)
    p = jnp.exp(s - m_key)
    m_all, l_all, g_all = m_ref[...], l_ref[...], g_ref[...]
    for j in range(blk_per_step):
        n = c * blk_per_step + j
        l_j = jnp.sum(jnp.where(key_blk == j, p, 0.0), axis=-1, keepdims=True)
        part = pages[per_blk * j][0:width, :] * qcol
        for u in range(1, per_blk):
            part = part + pages[per_blk * j + u][0:width, :] * qcol
        part = jnp.sum(part.reshape(MOBA_HEADS, HEAD_DIM // 8, 8, PAGE_SIZE), axis=1)
        acc = jnp.zeros((8, width), F32)
        for u in range(per_blk):
            pu = p[:, (per_blk * j + u) * PAGE_SIZE:(per_blk * j + u + 1) * PAGE_SIZE].astype(BF16)
            acc = acc + _dot_nt(pu, pages[per_blk * j + u][width:2 * width, :].astype(BF16))
        acc_ref[n] = acc
        m_all = jnp.where(lane == n, m_blk[j], m_all)
        l_all = jnp.where(lane == n, l_j, l_all)
        for h in range(MOBA_HEADS):
            g_h = jnp.sum(jnp.sum(part[h], axis=-1, keepdims=True), axis=0, keepdims=True) * (1.0 / MOBA_BLOCK)
            g_all = jnp.where((lane == n) & (row == h), g_h, g_all)
    m_ref[...] = m_all
    l_ref[...] = l_all
    g_ref[...] = g_all

    @pl.when(c == pl.num_programs(1) - 1)
    def _():
        kv_new = kvnew_ref[...]
        s_self = (jnp.sum(q * kv_new[:, :width], axis=-1, keepdims=True) * (1.0 / math.sqrt(HEAD_DIM))
                  + bias0_ref[...][:, :1])
        past_blk = lane < n_blocks
        score = jnp.where(past_blk, g_all, NEG_INF)
        sel = (_rank_before(score, n_blocks) < MOBA_TOPK) & past_blk
        m_tot = jnp.maximum(jnp.max(jnp.where(sel, m_all, NEG_INF), axis=-1, keepdims=True), s_self)
        w = jnp.where(sel, jnp.exp(m_all - m_tot), 0.0)
        e_self = jnp.exp(s_self - m_tot)
        den = jnp.sum(w * l_all, axis=-1, keepdims=True) + e_self
        num = e_self * kv_new[:, width:]
        for n in range(n_blocks):
            num = num + w[:, n:n + 1] * acc_ref[n]
        o = num / den
        row_w = lax.broadcasted_iota(jnp.int32, (8, width), 0)
        lane_w = lax.broadcasted_iota(jnp.int32, (8, width), 1)
        o_ref[...] = jnp.sum(jnp.where(lane_w // HEAD_DIM == row_w, o, 0.0), axis=0, keepdims=True).astype(o_ref.dtype)


def _moba_sample(cache_t, layer, page_table, q_bd, q_col, kv_new, bias, bias0):
    b, n_pages = page_table.shape
    width = MOBA_HEADS * HEAD_DIM
    past = n_pages * PAGE_SIZE
    assert past % MOBA_BLOCK == 0 and past // MOBA_BLOCK <= LANES
    n_blocks = past // MOBA_BLOCK
    pages_per_step = min(16, n_pages)
    steps = n_pages // pages_per_step
    grid_spec = pltpu.PrefetchScalarGridSpec(
        num_scalar_prefetch=1, grid=(b, steps),
        in_specs=_page_specs(pages_per_step, layer, (2 * width, PAGE_SIZE), lambda c: c * pages_per_step)
        + [pl.BlockSpec((None, 8, width), lambda i, c, pt: (i, 0, 0)),
           pl.BlockSpec((None, width, PAGE_SIZE), lambda i, c, pt: (i, 0, 0)),
           pl.BlockSpec((None, 1, 2 * width), lambda i, c, pt: (i, 0, 0)),
           pl.BlockSpec((8, pages_per_step * PAGE_SIZE), lambda i, c, pt: (0, c)),
           pl.BlockSpec((8, LANES), lambda i, c, pt: (0, 0))],
        out_specs=pl.BlockSpec((None, 1, width), lambda i, c, pt: (i, 0, 0)),
        scratch_shapes=[pltpu.VMEM((8, LANES), F32), pltpu.VMEM((8, LANES), F32), pltpu.VMEM((8, LANES), F32),
                        pltpu.VMEM((n_blocks, 8, width), F32)])
    return pl.pallas_call(
        functools.partial(_moba_sample_kernel, n_pages=pages_per_step, n_blocks=n_blocks),
        grid_spec=grid_spec,
        out_shape=jax.ShapeDtypeStruct((b, 1, width), BF16),
        compiler_params=_cparams("arbitrary", "arbitrary"),
    )(page_table, *([cache_t] * pages_per_step), q_bd, q_col, kv_new, bias, bias0)


def _nsa_sample_kernel(pt_ref, *refs, n_pages, n_slc):
    pages = refs[:n_pages]
    (q_ref, g_ref, new_ref, win_ref, w16_ref, pos_ref, w1f_ref, b1_ref, w2_ref, b2_ref, cbias_ref, sbias_ref,
     wbias_ref, bias0_ref, ov_ref, ind_ref, perm_ref, o_ref, wout_ref, cmp_ref, slc_ref) = refs[n_pages:]
    past = n_pages * PAGE_SIZE
    nsub = past // NSA_CMP_STRIDE
    hid = NSA_CMP_HIDDEN
    scale = 1.0 / math.sqrt(NSA_DK)
    nwin = win_ref.shape[1]
    two = 2 * NSA_DK
    st = NSA_CMP_STRIDE
    sub_pair = 2 * PAGE_SIZE // st
    for i in range(0, n_pages, 2):
        pair = jnp.concatenate([pages[i][...], pages[i + 1][...]], axis=-1)
        by_phase = _dot_nt(perm_ref[...], pair[:two, :].astype(BF16))
        for j in range(st):
            cmp_ref[j, i * PAGE_SIZE // st:i * PAGE_SIZE // st + sub_pair, :] = (
                by_phase[j * sub_pair:(j + 1) * sub_pair, :].astype(BF16))
        slc_ref[:, i * PAGE_SIZE:(i + 2) * PAGE_SIZE] = pair[two:, :].astype(BF16)

    q8 = q_ref[...].astype(BF16)
    q8f = q_ref[...]
    new = new_ref[...]
    head_row = lax.broadcasted_iota(jnp.int32, (8, 1), 0) < NSA_HEADS
    bias0 = bias0_ref[...][:, :1]

    hs = jnp.zeros((nsub, 4 * hid), F32)
    for j in range(NSA_CMP_STRIDE):
        hs = hs + _dot(cmp_ref[j], w16_ref[j])
    hs_new = _dot(jnp.broadcast_to(new[:, :two], (8, two)).astype(BF16), w16_ref[0])[0:1, :]
    sub_i = lax.broadcasted_iota(jnp.int32, (nsub, hid), 0)
    outs = []
    for c in range(2):
        first = hs[:, 2 * c * hid:(2 * c + 1) * hid]
        second = pltpu.roll(hs[:, (2 * c + 1) * hid:(2 * c + 2) * hid], nsub - 1, 0)
        second = jnp.where(sub_i == nsub - 1, hs_new[:, (2 * c + 1) * hid:(2 * c + 2) * hid], second)
        h = first + second + _hpos(pos_ref, w1f_ref, c) + b1_ref[c]
        outs.append(_dot(jax.nn.silu(h).astype(BF16), w2_ref[c]) + b2_ref[c])
    kvc = jnp.concatenate(outs, axis=-1).astype(BF16)

    lc = _dot_nt(q8, kvc) * scale + cbias_ref[...]
    c_end = lax.broadcasted_iota(jnp.int32, (8, nsub), 1) * NSA_CMP_STRIDE + NSA_CMP_LEN - 1
    p_c = _masked_softmax(lc, c_end <= past)
    o_c = _dot(p_c.astype(BF16), kvc)
    p_sum = jnp.broadcast_to(jnp.sum(jnp.where(head_row, p_c, 0.0), axis=0, keepdims=True), (8, nsub))
    imp = _dot(p_sum, ov_ref[...], precision=lax.Precision.HIGHEST)

    jj = lax.broadcasted_iota(jnp.int32, imp.shape, 1)
    cur = past // NSA_SLC_BLOCK
    s_ok = jj <= cur
    forced = (jj == 0) | (jj == cur) | (jj == cur - 1)
    score = jnp.where(forced, jnp.inf, jnp.where(s_ok, imp, NEG_INF))
    sel = ((_rank_before(score, n_slc) < min(NSA_TOPN, n_slc)) & s_ok).astype(BF16)

    slc = slc_ref[...]
    picked = _dot(sel, ind_ref[...])
    s_all = jnp.where(picked > 0.5, _dot(q8, slc) * scale + sbias_ref[...], NEG_INF)
    s_self = jnp.sum(q8f * new[:, two:2 * two], axis=-1, keepdims=True) * scale + bias0
    m = jnp.maximum(jnp.max(s_all, axis=-1, keepdims=True), s_self)
    p = jnp.exp(s_all - m)
    e_self = jnp.exp(s_self - m)
    den = jnp.sum(p, axis=-1, keepdims=True) + e_self
    o_s = (_dot_nt(p.astype(BF16), slc) + e_self * new[:, two:2 * two]) / den

    win = win_ref[...]
    winb = win.astype(BF16)
    lw = _dot(q8, winb) * scale + wbias_ref[...]
    d_w = nwin - lax.broadcasted_iota(jnp.int32, (8, nwin), 1)
    lw = jnp.where(d_w < NSA_WINDOW, lw, NEG_INF)
    lw_self = jnp.sum(q8f * new[:, 2 * two:3 * two], axis=-1, keepdims=True) * scale + bias0
    m = jnp.maximum(jnp.max(lw, axis=-1, keepdims=True), lw_self)
    p = jnp.exp(lw - m)
    e_self = jnp.exp(lw_self - m)
    den = jnp.sum(p, axis=-1, keepdims=True) + e_self
    o_w = (_dot_nt(p.astype(BF16), winb) + e_self * new[:, 2 * two:3 * two]) / den

    g = jax.nn.sigmoid(g_ref[...])
    o_ref[...] = (g[:, 0:1] * o_c + g[:, 1:2] * o_s + g[:, 2:3] * o_w).astype(o_ref.dtype)

    rr = lax.broadcasted_iota(jnp.int32, (two, two), 0)
    cc = lax.broadcasted_iota(jnp.int32, (two, two), 1)
    new_col = jnp.sum(jnp.where(rr == cc, jnp.broadcast_to(new[:, 2 * two:3 * two], (two, two)), 0.0),
                      axis=-1, keepdims=True)
    col = lax.broadcasted_iota(jnp.int32, (two, nwin), 1)
    wout_ref[...] = jnp.where(col == nwin - 1, new_col, pltpu.roll(win, nwin - 1, 1))


def _nsa_sample(cache_t, layer, page_table, win_t, q8, g8, new, w16, posf, w1f, b1, w2, b2,
                cbias, sbias, wbias, bias0, overlap, n_slc):
    b, n_pages = page_table.shape
    past = n_pages * PAGE_SIZE
    rows = cache_t.shape[2]
    nwin = win_t.shape[-1]
    ind = _block_indicator(overlap.shape[1], n_slc, NSA_SLC_BLOCK, past)
    assert n_pages % 2 == 0
    st = NSA_CMP_STRIDE
    sub_pair = 2 * PAGE_SIZE // st
    r = np.arange(2 * PAGE_SIZE)
    perm = jnp.asarray((r[None, :] == (r[:, None] % sub_pair) * st + r[:, None] // sub_pair).astype(np.float32),
                       dtype=BF16)
    full = lambda a: pl.BlockSpec(a.shape, lambda i, pt: (0,) * a.ndim)
    grid_spec = pltpu.PrefetchScalarGridSpec(
        num_scalar_prefetch=1, grid=(b,),
        in_specs=_page_specs(n_pages, layer, (rows, PAGE_SIZE), lambda: 0)
        + [pl.BlockSpec((None, 8, LANES), lambda i, pt: (i, 0, 0)),
           pl.BlockSpec((None, 8, LANES), lambda i, pt: (i, 0, 0)),
           pl.BlockSpec((None, 1, new.shape[-1]), lambda i, pt: (i, 0, 0)),
           pl.BlockSpec((None, None, 2 * NSA_DK, nwin), lambda i, pt: (layer, i, 0, 0)),
           full(w16), full(posf), full(w1f), full(b1), full(w2), full(b2),
           full(cbias), full(sbias), full(wbias), full(bias0), full(overlap), full(ind), full(perm)],
        out_specs=[pl.BlockSpec((None, 8, LANES), lambda i, pt: (i, 0, 0)),
                   pl.BlockSpec((None, 2 * NSA_DK, nwin), lambda i, pt: (i, 0, 0))],
        scratch_shapes=[pltpu.VMEM((st, past // st, 2 * NSA_DK), BF16), pltpu.VMEM((2 * NSA_DK, past), BF16)])
    return pl.pallas_call(
        functools.partial(_nsa_sample_kernel, n_pages=n_pages, n_slc=n_slc),
        grid_spec=grid_spec,
        out_shape=[jax.ShapeDtypeStruct((b, 8, LANES), F32),
                   jax.ShapeDtypeStruct((b, 2 * NSA_DK, nwin), F32)],
        compiler_params=_cparams("arbitrary"),
    )(page_table, *([cache_t] * n_pages), q8, g8, new, win_t, w16, posf, w1f, b1, w2, b2,
      cbias, sbias, wbias, bias0, overlap, ind, perm)


def _rot_half_cols(w):
    half = MLA_ROPE // 2
    return jnp.concatenate([-w[..., half:], w[..., :half]], axis=-1)


def _pack_w_in(w):
    d = w.shape[0]
    sizes = (256, 256, 256, 256, 384, 12, 256, 128, 32, 3 * d)
    offs = np.concatenate([[0], np.cumsum(sizes)])
    mq, mk, mv, nq, nkv, ng, cq, ckv, kpe, mg = [w[:, offs[i]:offs[i + 1]] for i in range(10)]
    z = lambda n: jnp.zeros((d, n), w.dtype)
    out = jnp.concatenate([mq, mk, mv, nq, nkv, ng, z(C_CQ - C_NG - 12), cq, ckv, kpe, _rot_half_cols(kpe),
                           z(C_MG - C_KPER - MLA_ROPE), mg], axis=1)
    assert out.shape[1] == N_PAD
    return out.astype(BF16)


def _pack_w16(w1):
    st = NSA_CMP_STRIDE
    blocks = []
    for c in range(2):
        wc = jnp.concatenate([w1[c, :st], w1[c, st:]], axis=-1)
        zc = jnp.zeros_like(wc)
        blocks.append(jnp.concatenate([wc, zc] if c == 0 else [zc, wc], axis=-1))
    return jnp.concatenate(blocks, axis=1).astype(BF16)


def _rope_tables(pos, n_rep):
    half = MLA_ROPE // 2
    freq = ROPE_THETA ** (-jnp.arange(half, dtype=F32) / half)
    ang = pos.astype(F32)[:, None] * freq[None, :]
    cos = jnp.tile(jnp.cos(ang), (1, 2 * n_rep))
    sin = jnp.tile(jnp.sin(ang), (1, 2 * n_rep))
    return cos, sin


def _overlap_matrix(nc, n_slc, n_cols):
    ci = np.arange(nc)[:, None] * NSA_CMP_STRIDE
    sj = np.arange(n_cols)[None, :] * NSA_SLC_BLOCK
    ov = (ci < sj + NSA_SLC_BLOCK) & (ci + NSA_CMP_LEN > sj) & (np.arange(n_cols)[None, :] < n_slc)
    return jnp.asarray(ov.astype(np.float32))


def _slot_minor_views(cache_moba_kv, cache_nsa_kv, cache_mla, state_nsa_win):
    d, pool = cache_moba_kv.shape[:2]
    moba_t = jnp.transpose(cache_moba_kv, (0, 1, 3, 4, 5, 2)).reshape(d, pool, -1, PAGE_SIZE)
    nsa_t = jnp.transpose(cache_nsa_kv, (0, 1, 3, 4, 2)).reshape(d, pool, -1, PAGE_SIZE)
    mla_t = jnp.transpose(cache_mla, (0, 1, 3, 2))
    nwin = state_nsa_win.shape[2]
    win_t = jnp.transpose(state_nsa_win, (0, 1, 3, 4, 2)).reshape(d, state_nsa_win.shape[1], -1, nwin)
    return moba_t, nsa_t, mla_t, win_t


def kernel(x_prompt, x_sample, cache_moba_kv, cache_nsa_kv, cache_mla, state_nsa_win, page_table,
           p_prompt, p_sample, t5_bias, norm_mix, w_in, mla_q_norm, mla_kv_norm, w_q_up, w_kv_up,
           nsa_cmp_w1, nsa_cmp_b1, nsa_cmp_w2, nsa_cmp_b2, nsa_cmp_pos, w_branch, w_out, norm_ffn,
           w_ffn_gate, w_ffn_up, w_ffn_down, norm_ple, w_ple_gate, w_ple_proj, norm_final):
    depth = w_in.shape[0]
    bsz, seq, d = x_prompt.shape
    dec_b, dec_s, _ = x_sample.shape
    n_pages = page_table.shape[1]
    past = n_pages * PAGE_SIZE
    nwin = state_nsa_win.shape[2]
    assert dec_s == 1 and nwin == NSA_WINDOW and d == C_MG // 2
    wp = min(NSA_WINDOW, seq)
    page_table = page_table.astype(jnp.int32)
    moba_t, nsa_t, mla_t, win_t = _slot_minor_views(cache_moba_kv, cache_nsa_kv, cache_mla, state_nsa_win)

    nt = seq // ATT_TILE
    n_delta = max(_n_distinct_deltas(ATT_TILE, nt), min(3, nt))
    tiles = _lookup(t5_bias, _toeplitz_buckets(ATT_TILE, n_delta), 1)
    bias_moba_tiles = tiles[:, :MOBA_HEADS]
    bias_nsa_tiles = tiles[:, MOBA_HEADS:]
    nc_p = seq // NSA_CMP_STRIDE
    c_end_p = np.arange(nc_p) * NSA_CMP_STRIDE + NSA_CMP_LEN - 1
    cbias_p = _lookup(t5_bias[:, MOBA_HEADS:], _t5_bucket_np(np.arange(seq)[:, None] - c_end_p[None, :]), 0)
    n_slc_p = -(-seq // NSA_SLC_BLOCK)
    overlap_p = _overlap_matrix(nc_p, n_slc_p, LANES)

    def head_rows(idx, lo):
        a = _lookup(t5_bias[:, lo:lo + 4], idx, 0)
        return jnp.concatenate([a, jnp.zeros_like(a)], axis=0)

    dist_bucket = _t5_bucket_np(past - np.arange(past))
    sbias_moba = head_rows(dist_bucket, 0)
    sbias_nsa = head_rows(dist_bucket, MOBA_HEADS)
    nsub_s = past // NSA_CMP_STRIDE
    c_end_s = np.arange(nsub_s) * NSA_CMP_STRIDE + NSA_CMP_LEN - 1
    cbias_s = head_rows(_t5_bucket_np(past - c_end_s), MOBA_HEADS)
    wbias_s = head_rows(_t5_bucket_np(nwin - np.arange(nwin)), MOBA_HEADS)
    zero_bucket = np.full((LANES,), _t5_bucket_np(np.array(0)), np.int32)
    bias0_moba = head_rows(zero_bucket, 0)
    bias0_nsa = head_rows(zero_bucket, MOBA_HEADS)
    n_slc_s = -(-(past + dec_s) // NSA_SLC_BLOCK)
    slc_cols = -(-n_slc_s // LANES) * LANES
    overlap_s = _overlap_matrix(nsub_s, n_slc_s, slc_cols)

    cos_p, sin_p = _rope_tables(jnp.arange(seq, dtype=jnp.int32), MLA_HEADS)
    cos_s, sin_s = _rope_tables(jnp.full((dec_b,), past, jnp.int32), MLA_HEADS)

    xp = x_prompt.reshape(bsz * seq, d)
    xs = x_sample.reshape(dec_b, d)
    outs = [[] for _ in range(8)]
    for l in range(depth):
        w_in_p = _pack_w_in(w_in[l])
        wq = w_q_up[l].reshape(MLA_Q_RANK, MLA_HEADS, MLA_NOPE + MLA_ROPE)
        wq_nope = wq[:, :, :MLA_NOPE].reshape(MLA_Q_RANK, -1).astype(BF16)
        wq_pe = wq[:, :, MLA_NOPE:]
        wq_per = _rot_half_cols(wq_pe).reshape(MLA_Q_RANK, -1).astype(BF16)
        wq_pe = wq_pe.reshape(MLA_Q_RANK, -1).astype(BF16)
        wk_abs = jnp.transpose(w_kv_up[l][:, :, :MLA_NOPE], (1, 2, 0)).astype(BF16)
        wuv = jnp.transpose(w_kv_up[l][:, :, MLA_NOPE:], (1, 0, 2)).astype(BF16)
        g_mix = norm_mix[l][None, :]
        qn = mla_q_norm[l][None, :]
        kn = mla_kv_norm[l][None, :]
        w16 = _pack_w16(nsa_cmp_w1[l])
        posf = nsa_cmp_pos[l].reshape(2, 1, NSA_CMP_LEN * NSA_DK)
        w1f = nsa_cmp_w1[l].reshape(2, NSA_CMP_LEN * NSA_DK, NSA_CMP_HIDDEN)
        b1 = nsa_cmp_b1[l][:, None, :]
        w2 = nsa_cmp_w2[l].astype(BF16)
        b2 = nsa_cmp_b2[l][:, None, :]
        wb = w_branch[l].astype(BF16)
        wo = w_out[l].astype(BF16)
        ffn_w = (norm_ffn[l][None, :], w_ffn_gate[l].astype(BF16), w_ffn_up[l].astype(BF16),
                 w_ffn_down[l].astype(BF16), norm_ple[l][None, :], w_ple_gate[l].astype(BF16),
                 w_ple_proj[l].astype(BF16))
        last = l == depth - 1

        z, mg = _inproj(xp, g_mix, w_in_p)
        z3 = z.reshape(bsz, seq, -1)
        q_cat, mla_row, k_cat = _mla_prep(z, bsz, seq, qn, kn, wq_nope, wq_pe, wq_per, wk_abs, cos_p, sin_p)
        o_m = _moba_prompt(z, bsz, seq, bias_moba_tiles)
        kvc = _nsa_compress_prompt(z, bsz, seq, w16, posf, w1f, b1, w2, b2)
        o_n = _nsa_prompt(z, bsz, seq, kvc, cbias_p, bias_nsa_tiles, overlap_p)
        o_l = _mla_attn_prompt(q_cat, k_cat)
        xp3 = _mixer_out(xp.reshape(bsz, seq, d), mg, o_m, o_n, o_l, wuv, wb, wo)
        xp = _ffn_ple(xp3.reshape(bsz * seq, d), *ffn_w, p_prompt[l].reshape(bsz * seq, -1),
                      norm_final[None, :], last)
        outs[0].append(z3[:, :, C_MK:C_MK + 512].reshape(bsz, seq, 2, MOBA_HEADS, HEAD_DIM))
        outs[2].append(z3[:, :, C_NKV:C_NKV + 4 * NSA_DK].reshape(bsz, seq, 4, NSA_DK))
        outs[4].append(mla_row)
        outs[6].append(z3[:, seq - wp:, C_NKV + 4 * NSA_DK:C_NKV + 6 * NSA_DK].reshape(bsz, wp, 2, NSA_DK))

        zs, mg_s = _inproj(xs, g_mix, w_in_p)
        q_cat_s, mla_row_s, _ = _mla_prep(zs, 1, dec_b, qn, kn, wq_nope, wq_pe, wq_per, wk_abs, cos_s, sin_s)
        mq = zs[:, C_MQ:C_MQ + 256]
        eye = jnp.eye(MOBA_HEADS, dtype=F32)[None, :, :, None]
        q_bd = (mq.reshape(dec_b, MOBA_HEADS, 1, HEAD_DIM) * eye).reshape(dec_b, MOBA_HEADS, MOBA_HEADS * HEAD_DIM)
        q_bd = jnp.concatenate([q_bd, jnp.zeros_like(q_bd)], axis=1)
        q_col = jnp.broadcast_to(mq[:, :, None], (dec_b, mq.shape[1], PAGE_SIZE))
        o_m = _moba_sample(moba_t, l, page_table, q_bd, q_col, zs[:, None, C_MK:C_MK + 512],
                           sbias_moba, bias0_moba)
        nq = zs[:, C_NQ:C_NQ + 256].reshape(dec_b, NSA_HEADS, NSA_DK)
        q8 = jnp.pad(nq, ((0, 0), (0, 8 - NSA_HEADS), (0, LANES - NSA_DK)))
        g8 = jnp.pad(zs[:, C_NG:C_NG + 3 * NSA_HEADS].reshape(dec_b, NSA_HEADS, 3),
                     ((0, 0), (0, 8 - NSA_HEADS), (0, LANES - 3)))
        o_n8, win_new_t = _nsa_sample(nsa_t, l, page_table, win_t, q8, g8,
                                      zs[:, None, C_NKV:C_NKV + 6 * NSA_DK], w16, posf, w1f, b1, w2, b2,
                                      cbias_s, sbias_nsa, wbias_s, bias0_nsa, overlap_s, n_slc_s)
        o_n = o_n8[:, :NSA_HEADS, NSA_DK:].reshape(dec_b, NSA_HEADS * NSA_DK).astype(BF16)
        o_l = _mla_sample(mla_t, l, page_table, jnp.transpose(q_cat_s[0], (1, 0, 2)),
                          mla_row_s.reshape(dec_b, 1, -1))
        xs3 = _mixer_out(xs[None], mg_s, o_m.reshape(1, dec_b, -1), o_n[None],
                         jnp.transpose(o_l, (1, 0, 2))[None], wuv, wb, wo)
        xs = _ffn_ple(xs3[0], *ffn_w, p_sample[l].reshape(dec_b, -1), norm_final[None, :], last)
        outs[1].append(zs[:, C_MK:C_MK + 512].reshape(dec_b, 1, 2, MOBA_HEADS, HEAD_DIM))
        outs[3].append(zs[:, C_NKV:C_NKV + 4 * NSA_DK].reshape(dec_b, 1, 4, NSA_DK))
        outs[5].append(mla_row_s.reshape(dec_b, 1, -1))
        outs[7].append(jnp.transpose(win_new_t.reshape(dec_b, 2, NSA_DK, nwin), (0, 3, 1, 2)))

    return (xp.reshape(bsz, seq, d), xs.reshape(dec_b, dec_s, d)) + tuple(jnp.stack(o) for o in outs)
```

```python
import functools
import math

import numpy as np
import jax
import jax.numpy as jnp
from jax import lax
from jax.experimental import pallas as pl
from jax.experimental.pallas import tpu as pltpu

PAGE_SIZE = 128
HEAD_DIM = 64
MOBA_HEADS = 4
MOBA_BLOCK = 256
MOBA_TOPK = 3
NSA_HEADS = 4
NSA_DK = 64
NSA_CMP_LEN = 32
NSA_CMP_STRIDE = 16
NSA_CMP_HIDDEN = 128
NSA_SLC_BLOCK = 64
NSA_TOPN = 16
NSA_WINDOW = 512
MLA_HEADS = 8
MLA_Q_RANK = 256
MLA_KV_RANK = 128
MLA_NOPE = 64
MLA_ROPE = 32
MLA_V = 64
ROPE_THETA = 10000.0
T5_BUCKETS = 32
T5_MAX_DIST = 128
EPS = 1e-6

LANES = 128
ATT_TILE = 256
KEY_STEP = 256
VMEM_LIMIT = 56 * 1024 * 1024

C_MQ, C_MK, C_MV, C_NQ, C_NKV, C_NG = 0, 256, 512, 768, 1024, 1408
C_CQ, C_CKV, C_KPE, C_KPER, C_MG, N_PAD = 1536, 1792, 1920, 1952, 2048, 5120

BF16 = jnp.bfloat16
F32 = jnp.float32
NEG_INF = float("-inf")
MLA_SCALE = 1.0 / math.sqrt(MLA_NOPE + MLA_ROPE)


def _cparams(*sem):
    return pltpu.CompilerParams(dimension_semantics=sem, vmem_limit_bytes=VMEM_LIMIT)


def _rms(x, g):
    return x * lax.rsqrt(jnp.mean(x * x, axis=-1, keepdims=True) + EPS) * g


def _dot(a, b, precision=None):
    return jnp.dot(a, b, preferred_element_type=F32, precision=precision)


def _dot_nt(a, b, precision=None):
    return lax.dot_general(a, b, (((1,), (1,)), ((), ())), preferred_element_type=F32, precision=precision)


def _masked_softmax(s, mask):
    s = jnp.where(mask, s, NEG_INF)
    m = jnp.max(s, axis=-1, keepdims=True)
    m = jnp.where(m == NEG_INF, 0.0, m)
    e = jnp.exp(s - m)
    return e / jnp.maximum(jnp.sum(e, axis=-1, keepdims=True), 1e-30)


def _exp_rows(s):
    e = jnp.exp(s - jnp.max(s, axis=-1, keepdims=True))
    return e, jnp.sum(e, axis=-1, keepdims=True)


def _rank_before(score, n):
    lane = lax.broadcasted_iota(jnp.int32, score.shape, score.ndim - 1)
    rank = jnp.zeros(score.shape, jnp.int32)
    for m in range(n):
        sm = score[..., m:m + 1]
        before = (sm > score) | ((sm == score) & (m < lane))
        rank = rank + before.astype(jnp.int32)
    return rank


def _picked_rows(score_t, n, k, ok_t):
    r, t = score_t.shape
    row = lax.broadcasted_iota(jnp.int32, (r, t), 0)
    rank = jnp.zeros((r, t), jnp.int32)
    for m in range(n):
        sm = score_t[m:m + 1, :]
        rank = rank + ((sm > score_t) | ((sm == score_t) & (m < row))).astype(jnp.int32)
    sel_t = jnp.where((rank < k) & ok_t, 1.0, 0.0)
    if r < LANES:
        sel_t = jnp.concatenate([sel_t, jnp.zeros((LANES - r, t), F32)], axis=0)
    return sel_t.T.astype(BF16)


def _t5_bucket_np(dist):
    n = np.maximum(dist, 0)
    max_exact = T5_BUCKETS // 2
    nf = np.maximum(n, 1).astype(np.float32)
    large = max_exact + (np.log(nf / max_exact) / math.log(T5_MAX_DIST / max_exact)
                         * (T5_BUCKETS - max_exact)).astype(np.int32)
    return np.where(n < max_exact, n, np.minimum(large, T5_BUCKETS - 1)).astype(np.int32)


def _toeplitz_buckets(tile, n_delta):
    i = np.arange(tile)[:, None]
    j = np.arange(tile)[None, :]
    return np.stack([_t5_bucket_np(d * tile + i - j) for d in range(n_delta)])


def _n_distinct_deltas(tile, n_tiles):
    for n in range(1, n_tiles + 1):
        lo = (n - 1) * tile - (tile - 1)
        if lo >= 0 and np.all(_t5_bucket_np(np.arange(lo, n_tiles * tile)) == _t5_bucket_np(np.array(lo))):
            return n
    return n_tiles


def _lookup(table, idx, head_axis):
    idx = jnp.expand_dims(jnp.asarray(idx), head_axis)
    shape = [1] * idx.ndim
    shape[head_axis] = table.shape[1]
    out = jnp.zeros(np.broadcast_shapes(idx.shape, tuple(shape)), F32)
    for b in range(table.shape[0]):
        out = jnp.where(idx == b, table[b].reshape(shape), out)
    return out


def _block_indicator(rows, n_blocks, block, width):
    j = np.arange(rows)[:, None]
    k = np.arange(width)[None, :]
    return jnp.asarray(((k // block == j) & (j < n_blocks)).astype(np.float32), dtype=BF16)


def _inproj_kernel(x_ref, g_ref, w_ref, o_ref, mg_ref):
    xn = _rms(x_ref[...], g_ref[...]).astype(BF16)
    z = _dot(xn, w_ref[...])
    o_ref[...] = z[:, :C_MG]
    mg_ref[...] = z[:, C_MG:].astype(BF16)


def _inproj(x, g, w):
    m = x.shape[0]
    tm = min(256, m)
    n_gate = w.shape[1] - C_MG
    return pl.pallas_call(
        _inproj_kernel,
        grid=(m // tm,),
        in_specs=[pl.BlockSpec((tm, x.shape[1]), lambda i: (i, 0)),
                  pl.BlockSpec((1, x.shape[1]), lambda i: (0, 0)),
                  pl.BlockSpec(w.shape, lambda i: (0, 0))],
        out_specs=[pl.BlockSpec((tm, C_MG), lambda i: (i, 0)), pl.BlockSpec((tm, n_gate), lambda i: (i, 0))],
        out_shape=[jax.ShapeDtypeStruct((m, C_MG), F32), jax.ShapeDtypeStruct((m, n_gate), BF16)],
        compiler_params=_cparams("parallel"),
    )(x, g, w)


def _mla_prep_kernel(cq_ref, ck_ref, qn_ref, kn_ref, wn_ref, wp_ref, wpr_ref, wk_ref, cos_ref, sin_ref,
                     qcat_ref, row_ref, kcat_ref):
    tm = cq_ref.shape[0]
    cqn = _rms(cq_ref[...], qn_ref[...]).astype(BF16)
    q_nope = _dot(cqn, wn_ref[...])
    cos = cos_ref[...]
    sin = sin_ref[...]
    q_pe = _dot(cqn, wp_ref[...]) * cos + _dot(cqn, wpr_ref[...]) * sin
    ck = ck_ref[...]
    ckv_n = _rms(ck[:, :MLA_KV_RANK], kn_ref[...])
    k_pe = (ck[:, MLA_KV_RANK:MLA_KV_RANK + MLA_ROPE] * cos[:, :MLA_ROPE]
            + ck[:, MLA_KV_RANK + MLA_ROPE:MLA_KV_RANK + 2 * MLA_ROPE] * sin[:, :MLA_ROPE])
    row_ref[...] = jnp.concatenate([ckv_n, k_pe], axis=-1)
    pad = jnp.zeros((tm, 2 * LANES - MLA_KV_RANK - MLA_ROPE), F32)
    kcat_ref[...] = jnp.concatenate([ckv_n, k_pe, pad], axis=-1).astype(BF16)
    for h in range(MLA_HEADS):
        q_lat = _dot(q_nope[:, h * MLA_NOPE:(h + 1) * MLA_NOPE].astype(BF16), wk_ref[h])
        q_h = jnp.concatenate([q_lat, q_pe[:, h * MLA_ROPE:(h + 1) * MLA_ROPE], pad], axis=-1)
        qcat_ref[h] = (q_h * MLA_SCALE).astype(BF16)


def _mla_prep(z, bv, sv, qn, kn, wn, wp, wpr, wk, cos, sin):
    z3 = z.reshape(bv, sv, -1)
    tm = min(256, sv)
    full = lambda a: pl.BlockSpec(a.shape, lambda b, i: (0,) * a.ndim)
    return pl.pallas_call(
        _mla_prep_kernel,
        grid=(bv, sv // tm),
        in_specs=[pl.BlockSpec((None, tm, 256), lambda b, i: (b, i, C_CQ // 256)),
                  pl.BlockSpec((None, tm, 256), lambda b, i: (b, i, C_CKV // 256)),
                  full(qn), full(kn), full(wn), full(wp), full(wpr), full(wk),
                  pl.BlockSpec((tm, 256), lambda b, i: (i, 0)),
                  pl.BlockSpec((tm, 256), lambda b, i: (i, 0))],
        out_specs=[pl.BlockSpec((None, MLA_HEADS, tm, 256), lambda b, i: (b, 0, i, 0)),
                   pl.BlockSpec((None, tm, MLA_KV_RANK + MLA_ROPE), lambda b, i: (b, i, 0)),
                   pl.BlockSpec((None, tm, 256), lambda b, i: (b, i, 0))],
        out_shape=[jax.ShapeDtypeStruct((bv, MLA_HEADS, sv, 256), BF16),
                   jax.ShapeDtypeStruct((bv, sv, MLA_KV_RANK + MLA_ROPE), F32),
                   jax.ShapeDtypeStruct((bv, sv, 256), BF16)],
        compiler_params=_cparams("parallel", "parallel"),
    )(z3, z3, qn, kn, wn, wp, wpr, wk, cos, sin)


def _mixer_out_kernel(om_ref, on_ref, ol_ref, g0_ref, g1_ref, g2_ref, x_ref, wuv_ref, wb_ref, wo_ref, o_ref):
    o_mla = jnp.concatenate([_dot(ol_ref[h], wuv_ref[h]) for h in range(MLA_HEADS)], axis=-1).astype(BF16)
    n_a = om_ref.shape[-1]
    n_b = n_a + on_ref.shape[-1]
    b_moba = _dot(om_ref[...], wb_ref[:n_a, :])
    b_nsa = _dot(on_ref[...], wb_ref[n_a:n_b, :])
    b_mla = _dot(o_mla, wb_ref[n_b:, :])
    h = (jax.nn.sigmoid(g0_ref[...].astype(F32)) * b_moba + jax.nn.sigmoid(g1_ref[...].astype(F32)) * b_nsa
         + jax.nn.sigmoid(g2_ref[...].astype(F32)) * b_mla)
    o_ref[...] = x_ref[...] + _dot(h.astype(BF16), wo_ref[...])


def _mixer_out(x, mg, o_moba, o_nsa, o_lat, wuv, wb, wo):
    bv, sv, d = x.shape
    z3 = mg.reshape(bv, sv, -1)
    tm = min(256, sv)
    full = lambda a: pl.BlockSpec(a.shape, lambda b, i: (0,) * a.ndim)
    gate = lambda k: pl.BlockSpec((None, tm, d), lambda b, i: (b, i, k))
    row = lambda w: pl.BlockSpec((None, tm, w), lambda b, i: (b, i, 0))
    return pl.pallas_call(
        _mixer_out_kernel,
        grid=(bv, sv // tm),
        in_specs=[row(o_moba.shape[-1]), row(o_nsa.shape[-1]),
                  pl.BlockSpec((None, MLA_HEADS, tm, MLA_KV_RANK), lambda b, i: (b, 0, i, 0)),
                  gate(0), gate(1), gate(2), row(d), full(wuv), full(wb), full(wo)],
        out_specs=row(d),
        out_shape=jax.ShapeDtypeStruct(x.shape, F32),
        compiler_params=_cparams("parallel", "parallel"),
    )(o_moba, o_nsa, o_lat, z3, z3, z3, x, wuv, wb, wo)


def _ffn_ple_kernel(x_ref, gf_ref, wg_ref, wu_ref, wd_ref, gp_ref, wpg_ref, wpp_ref, p_ref, gl_ref,
                    o_ref, xn_ref, acc_ref, *, final_norm):
    f = pl.program_id(1)

    @pl.when(f == 0)
    def _():
        xn_ref[...] = _rms(x_ref[...], gf_ref[...]).astype(BF16)
        acc_ref[...] = jnp.zeros_like(acc_ref)

    xn = xn_ref[...]
    a = jax.nn.silu(_dot(xn, wg_ref[...])) * _dot(xn, wu_ref[...])
    acc_ref[...] += _dot(a.astype(BF16), wd_ref[...])

    @pl.when(f == pl.num_programs(1) - 1)
    def _():
        x2 = x_ref[...] + acc_ref[...]
        gate = jax.nn.sigmoid(_dot(_rms(x2, gp_ref[...]).astype(BF16), wpg_ref[...]))
        x3 = x2 + gate * _dot(p_ref[...].astype(BF16), wpp_ref[...])
        o_ref[...] = _rms(x3, gl_ref[...]) if final_norm else x3


def _ffn_ple(x, gf, wg, wu, wd, gp, wpg, wpp, p, gl, final_norm):
    m, d = x.shape
    dff = wg.shape[1]
    tm = min(512, m)
    tf = dff // 2 if (dff // 2) % LANES == 0 else dff
    full = lambda a: pl.BlockSpec(a.shape, lambda i, f: (0,) * a.ndim)
    return pl.pallas_call(
        functools.partial(_ffn_ple_kernel, final_norm=final_norm),
        grid=(m // tm, dff // tf),
        in_specs=[pl.BlockSpec((tm, d), lambda i, f: (i, 0)), full(gf),
                  pl.BlockSpec((d, tf), lambda i, f: (0, f)),
                  pl.BlockSpec((d, tf), lambda i, f: (0, f)),
                  pl.BlockSpec((tf, d), lambda i, f: (f, 0)),
                  full(gp), full(wpg), full(wpp),
                  pl.BlockSpec((tm, p.shape[1]), lambda i, f: (i, 0)), full(gl)],
        out_specs=pl.BlockSpec((tm, d), lambda i, f: (i, 0)),
        out_shape=jax.ShapeDtypeStruct((m, d), F32),
        scratch_shapes=[pltpu.VMEM((tm, d), BF16), pltpu.VMEM((tm, d), F32)],
        compiler_params=_cparams("parallel", "arbitrary"),
    )(x, gf, wg, wu, wd, gp, wpg, wpp, p, gl)


def _for_prefix_width(q_end, step, total, body):
    for v in range(total // step):
        pl.when((q_end - 1) // step == v)(functools.partial(body, (v + 1) * step))


def _mla_attn_kernel(q_ref, k_ref, o_ref, *, tq, step):
    qi = pl.program_id(1)
    rows = MLA_HEADS * tq
    q = q_ref[...].reshape(rows, q_ref.shape[-1])
    t_pos = qi * tq + lax.broadcasted_iota(jnp.int32, (MLA_HEADS, tq, 1), 1).reshape(rows, 1)

    def body(width):
        k = k_ref[0:width, :]
        s = _dot_nt(q, k)
        k_pos = lax.broadcasted_iota(jnp.int32, (rows, width), 1)
        e, den = _exp_rows(jnp.where(k_pos <= t_pos, s, NEG_INF))
        o = _dot(e.astype(BF16), k[:, :MLA_KV_RANK]) / den
        o_ref[...] = o.reshape(MLA_HEADS, tq, MLA_KV_RANK).astype(o_ref.dtype)

    _for_prefix_width((qi + 1) * tq, step, k_ref.shape[0], body)


def _mla_attn_prompt(q_cat, k_cat):
    b, nh, s, w = q_cat.shape
    tq = min(128, s)
    step = min(KEY_STEP, s)
    return pl.pallas_call(
        functools.partial(_mla_attn_kernel, tq=tq, step=step),
        grid=(b, s // tq),
        in_specs=[pl.BlockSpec((None, nh, tq, w), lambda b, i: (b, 0, i, 0)),
                  pl.BlockSpec((None, s, w), lambda b, i: (b, 0, 0))],
        out_specs=pl.BlockSpec((None, nh, tq, MLA_KV_RANK), lambda b, i: (b, 0, i, 0)),
        out_shape=jax.ShapeDtypeStruct((b, nh, s, MLA_KV_RANK), BF16),
        compiler_params=_cparams("parallel", "arbitrary"),
    )(q_cat, k_cat)


def _bias_row(bias_ref, h, qi, first_tile, n_tiles, n_delta):
    return jnp.concatenate([bias_ref[jnp.clip(qi - first_tile - n, 0, n_delta - 1), h] for n in range(n_tiles)],
                           axis=-1)


def _moba_prompt_kernel(q_ref, k_ref, v_ref, bias_ref, ind_ref, o_ref, kmean_ref, kb_ref, vb_ref,
                        *, nb, n_delta, step):
    t = ATT_TILE
    qi = pl.program_id(1)
    width_all = MOBA_HEADS * HEAD_DIM
    pair = 2 * HEAD_DIM

    @pl.when(qi == 0)
    def _():
        kb_ref[...] = k_ref[...].astype(BF16)
        vb_ref[...] = v_ref[...].astype(BF16)
        kmean_ref[...] = jnp.zeros_like(kmean_ref)
        for n in range(nb):
            kmean_ref[n:n + 1, :] = jnp.mean(k_ref[n * t:(n + 1) * t, :], axis=0, keepdims=True)

    q = q_ref[...]
    lane = lax.broadcasted_iota(jnp.int32, (t, width_all), 1)
    rows_nb = -(-nb // 8) * 8
    blk_t = lax.broadcasted_iota(jnp.int32, (rows_nb, t), 0)
    lane2 = lax.broadcasted_iota(jnp.int32, (t, pair), 1)
    kmean = kmean_ref[0:rows_nb, :]
    sels, q2s = [], []
    for h in range(MOBA_HEADS):
        q_h = jnp.where((lane >= h * HEAD_DIM) & (lane < (h + 1) * HEAD_DIM), q, 0.0)
        gate_t = _dot_nt(kmean, q_h, precision=lax.Precision.HIGHEST)
        past_t = blk_t < qi
        sels.append(_picked_rows(jnp.where(past_t, gate_t, NEG_INF), nb, MOBA_TOPK, past_t))
        hp, half = h // 2, h % 2
        q2 = q[:, hp * pair:(hp + 1) * pair]
        q2s.append(jnp.where((lane2 >= half * HEAD_DIM) & (lane2 < (half + 1) * HEAD_DIM), q2, 0.0).astype(BF16))

    def body(width):
        q_pos = qi * t + lax.broadcasted_iota(jnp.int32, (t, width), 0)
        k_pos = lax.broadcasted_iota(jnp.int32, (t, width), 1)
        own = (k_pos >= qi * t) & (k_pos <= q_pos)
        outs = []
        for h in range(MOBA_HEADS):
            hp, half = h // 2, h % 2
            kb = kb_ref[0:width, hp * pair:(hp + 1) * pair]
            vb = vb_ref[0:width, hp * pair:(hp + 1) * pair]
            picked = _dot(sels[h], ind_ref[:, 0:width])
            mask = jnp.where(own, 1.0, picked) > 0.5
            s = (_dot_nt(q2s[h], kb) * (1.0 / math.sqrt(HEAD_DIM))
                 + _bias_row(bias_ref, h, qi, 0, width // t, n_delta))
            e, den = _exp_rows(jnp.where(mask, s, NEG_INF))
            o2 = _dot(e.astype(BF16), vb) / den
            outs.append(o2[:, half * HEAD_DIM:(half + 1) * HEAD_DIM])
        o_ref[...] = jnp.concatenate(outs, axis=-1).astype(o_ref.dtype)

    _for_prefix_width((qi + 1) * t, step, k_ref.shape[0], body)


def _moba_prompt(z, b, s, bias_tiles):
    assert s % ATT_TILE == 0 and MOBA_BLOCK == ATT_TILE and s // MOBA_BLOCK <= LANES
    z3 = z.reshape(b, s, -1)
    nb = s // MOBA_BLOCK
    width = MOBA_HEADS * HEAD_DIM
    ind = _block_indicator(LANES, nb, MOBA_BLOCK, s)
    full = lambda a: pl.BlockSpec(a.shape, lambda b, i: (0,) * a.ndim)
    return pl.pallas_call(
        functools.partial(_moba_prompt_kernel, nb=nb, n_delta=bias_tiles.shape[0], step=min(KEY_STEP, s)),
        grid=(b, nb),
        in_specs=[pl.BlockSpec((None, ATT_TILE, width), lambda b, i: (b, i, C_MQ // width)),
                  pl.BlockSpec((None, s, width), lambda b, i: (b, 0, C_MK // width)),
                  pl.BlockSpec((None, s, width), lambda b, i: (b, 0, C_MV // width)),
                  full(bias_tiles), full(ind)],
        out_specs=pl.BlockSpec((None, ATT_TILE, width), lambda b, i: (b, i, 0)),
        out_shape=jax.ShapeDtypeStruct((b, s, width), BF16),
        scratch_shapes=[pltpu.VMEM((LANES, width), F32), pltpu.VMEM((s, width), BF16),
                        pltpu.VMEM((s, width), BF16)],
        compiler_params=_cparams("parallel", "arbitrary"),
    )(z3, z3, z3, bias_tiles, ind)


def _hpos(pos_ref, w1f_ref, c):
    pos = jnp.broadcast_to(pos_ref[c], (8, pos_ref.shape[-1]))
    return _dot(pos, w1f_ref[c])[0:1, :]


def _nsa_compress_prompt_kernel(z_ref, w16_ref, pos_ref, w1f_ref, b1_ref, w2_ref, b2_ref, o_ref, *, nsub):
    hid = NSA_CMP_HIDDEN
    hs = jnp.zeros((nsub, 4 * hid), F32)
    for j in range(NSA_CMP_STRIDE):
        r = z_ref[pl.ds(j, nsub, stride=NSA_CMP_STRIDE), :]
        hs = hs + _dot(r.astype(BF16), w16_ref[j])
    outs = []
    for c in range(2):
        first = hs[:, 2 * c * hid:(2 * c + 1) * hid]
        second = pltpu.roll(hs[:, (2 * c + 1) * hid:(2 * c + 2) * hid], nsub - 1, 0)
        h = first + second + _hpos(pos_ref, w1f_ref, c) + b1_ref[c]
        outs.append(_dot(jax.nn.silu(h).astype(BF16), w2_ref[c]) + b2_ref[c])
    o_ref[...] = jnp.concatenate(outs, axis=-1)


def _nsa_compress_prompt(z, b, s, w16, posf, w1f, b1, w2, b2):
    z3 = z.reshape(b, s, -1)
    nsub = s // NSA_CMP_STRIDE
    full = lambda a: pl.BlockSpec(a.shape, lambda i: (0,) * a.ndim)
    return pl.pallas_call(
        functools.partial(_nsa_compress_prompt_kernel, nsub=nsub),
        grid=(b,),
        in_specs=[pl.BlockSpec((None, s, LANES), lambda i: (i, 0, C_NKV // LANES)),
                  full(w16), full(posf), full(w1f), full(b1), full(w2), full(b2)],
        out_specs=pl.BlockSpec((None, nsub, 2 * NSA_DK), lambda i: (i, 0, 0)),
        out_shape=jax.ShapeDtypeStruct((b, nsub, 2 * NSA_DK), F32),
        compiler_params=_cparams("parallel"),
    )(z3, w16, posf, w1f, b1, w2, b2)


def _nsa_prompt_kernel(q_ref, g_ref, kvc_ref, slc_ref, win_ref, cbias_ref, bias_ref, ovt_ref, ind_ref, o_ref,
                       *, n_delta, n_slc, step, win_keys):
    t = ATT_TILE
    nh = NSA_HEADS
    qi = pl.program_id(1)
    scale = 1.0 / math.sqrt(NSA_DK)
    q = q_ref[...]
    zero = jnp.zeros((t, NSA_DK), F32)
    qs = [jnp.concatenate([q[:, h * NSA_DK:(h + 1) * NSA_DK], zero], axis=-1).astype(BF16) for h in range(nh)]
    t_pos = qi * t + lax.broadcasted_iota(jnp.int32, (t, 1), 0)

    kvc = kvc_ref[...].astype(BF16)
    nc = kvc.shape[0]
    c_ok = lax.broadcasted_iota(jnp.int32, (t, nc), 1) * NSA_CMP_STRIDE + NSA_CMP_LEN - 1 <= t_pos
    o_c = []
    p_sum = jnp.zeros((t, nc), F32)
    for h in range(nh):
        p_c = _masked_softmax(_dot_nt(qs[h], kvc) * scale + cbias_ref[h], c_ok)
        o_c.append(_dot(p_c.astype(BF16), kvc))
        p_sum = p_sum + p_c
    rows_slc = -(-n_slc // 8) * 8
    imp_t = _dot_nt(ovt_ref[0:rows_slc, :], p_sum, precision=lax.Precision.HIGHEST)

    jj = lax.broadcasted_iota(jnp.int32, (rows_slc, t), 0)
    cur = (qi * t + lax.broadcasted_iota(jnp.int32, (rows_slc, t), 1)) // NSA_SLC_BLOCK
    s_ok = jj <= cur
    forced = (jj == 0) | (jj == cur) | (jj == cur - 1)
    score = jnp.where(forced, jnp.inf, jnp.where(s_ok, imp_t, NEG_INF))
    sel = _picked_rows(score, n_slc, min(NSA_TOPN, n_slc), s_ok)

    start = pl.multiple_of(jnp.maximum((qi + 1) * t - win_keys, 0), t)
    wb = win_ref[pl.ds(start, win_keys), :].astype(BF16)
    d_w = t_pos - (start + lax.broadcasted_iota(jnp.int32, (t, win_keys), 1))
    w_ok = (d_w >= 0) & (d_w < NSA_WINDOW)
    o_w = []
    for h in range(nh):
        s = _dot_nt(qs[h], wb) * scale + _bias_row(bias_ref, h, qi, start // t, win_keys // t, n_delta)
        e, den = _exp_rows(jnp.where(w_ok, s, NEG_INF))
        o_w.append(_dot(e.astype(BF16), wb) / den)

    g = jax.nn.sigmoid(g_ref[...])

    def body(width):
        kb = slc_ref[0:width, :].astype(BF16)
        picked = _dot(sel, ind_ref[:, 0:width])
        k_pos = lax.broadcasted_iota(jnp.int32, (t, width), 1)
        mask = jnp.where(k_pos <= t_pos, picked, 0.0) > 0.5
        outs = []
        for h in range(nh):
            s = _dot_nt(qs[h], kb) * scale + _bias_row(bias_ref, h, qi, 0, width // t, n_delta)
            e, den = _exp_rows(jnp.where(mask, s, NEG_INF))
            o_s = _dot(e.astype(BF16), kb) / den
            o_h = (g[:, 3 * h:3 * h + 1] * o_c[h] + g[:, 3 * h + 1:3 * h + 2] * o_s
                   + g[:, 3 * h + 2:3 * h + 3] * o_w[h])
            outs.append(o_h[:, NSA_DK:])
        o_ref[...] = jnp.concatenate(outs, axis=-1).astype(o_ref.dtype)

    _for_prefix_width((qi + 1) * t, step, slc_ref.shape[0], body)


def _nsa_prompt(z, b, s, kvc, cbias, bias_tiles, overlap):
    assert s % ATT_TILE == 0
    z3 = z.reshape(b, s, -1)
    nt = s // ATT_TILE
    nc = kvc.shape[1]
    n_slc = -(-s // NSA_SLC_BLOCK)
    assert n_slc <= LANES
    width = NSA_HEADS * NSA_DK
    win_keys = min(NSA_WINDOW + ATT_TILE, s)
    ind = _block_indicator(LANES, n_slc, NSA_SLC_BLOCK, s)
    overlap_t = overlap.T
    full = lambda a: pl.BlockSpec(a.shape, lambda b, i: (0,) * a.ndim)
    seq = lambda col: pl.BlockSpec((None, s, LANES), lambda b, i: (b, 0, col // LANES))
    return pl.pallas_call(
        functools.partial(_nsa_prompt_kernel, n_delta=bias_tiles.shape[0], n_slc=n_slc, step=min(KEY_STEP, s),
                          win_keys=win_keys),
        grid=(b, nt),
        in_specs=[pl.BlockSpec((None, ATT_TILE, width), lambda b, i: (b, i, C_NQ // width)),
                  pl.BlockSpec((None, ATT_TILE, LANES), lambda b, i: (b, i, C_NG // LANES)),
                  pl.BlockSpec((None, nc, 2 * NSA_DK), lambda b, i: (b, 0, 0)),
                  seq(C_NKV + 2 * NSA_DK), seq(C_NKV + 4 * NSA_DK),
                  pl.BlockSpec((NSA_HEADS, ATT_TILE, nc), lambda b, i: (0, i, 0)),
                  full(bias_tiles), full(overlap_t), full(ind)],
        out_specs=pl.BlockSpec((None, ATT_TILE, width), lambda b, i: (b, i, 0)),
        out_shape=jax.ShapeDtypeStruct((b, s, width), BF16),
        compiler_params=_cparams("parallel", "parallel"),
    )(z3, z3, kvc, z3, z3, cbias, bias_tiles, overlap_t, ind)


def _page_specs(n, layer, shape, first_page):
    def spec(i):
        return pl.BlockSpec((None, None) + shape, lambda b, *a: (layer, a[-1][b, first_page(*a[:-1]) + i], 0, 0))
    return [spec(i) for i in range(n)]


def _mla_sample_kernel(pt_ref, *refs, n_pages):
    pages = refs[:n_pages]
    q_ref, knew_ref, o_ref, kbuf_ref = refs[n_pages:]
    width = MLA_KV_RANK + MLA_ROPE
    past = n_pages * PAGE_SIZE

    @pl.when(pl.program_id(0) == 0)
    def _():
        kbuf_ref[width:, :] = jnp.zeros((kbuf_ref.shape[0] - width, past), BF16)

    for i in range(n_pages):
        kbuf_ref[0:width, i * PAGE_SIZE:(i + 1) * PAGE_SIZE] = pages[i][...].astype(BF16)
    q = q_ref[...]
    k_new = knew_ref[...]
    kbuf = kbuf_ref[...]
    s = _dot(q, kbuf)
    s_new = jnp.sum(q[:, :width].astype(F32) * k_new, axis=-1, keepdims=True)
    m = jnp.maximum(jnp.max(s, axis=-1, keepdims=True), s_new)
    p = jnp.exp(s - m)
    p_new = jnp.exp(s_new - m)
    den = jnp.sum(p, axis=-1, keepdims=True) + p_new
    o = _dot_nt(p.astype(BF16), kbuf[:MLA_KV_RANK, :]) + p_new * k_new[:, :MLA_KV_RANK]
    o_ref[...] = (o / den).astype(o_ref.dtype)


def _mla_sample(cache_t, layer, page_table, q_cat, row_new):
    b, n_pages = page_table.shape
    width = cache_t.shape[2]
    grid_spec = pltpu.PrefetchScalarGridSpec(
        num_scalar_prefetch=1, grid=(b,),
        in_specs=_page_specs(n_pages, layer, (width, PAGE_SIZE), lambda: 0)
        + [pl.BlockSpec((None, MLA_HEADS, q_cat.shape[-1]), lambda i, pt: (i, 0, 0)),
           pl.BlockSpec((None, 1, width), lambda i, pt: (i, 0, 0))],
        out_specs=pl.BlockSpec((None, MLA_HEADS, MLA_KV_RANK), lambda i, pt: (i, 0, 0)),
        scratch_shapes=[pltpu.VMEM((q_cat.shape[-1], n_pages * PAGE_SIZE), BF16)])
    return pl.pallas_call(
        functools.partial(_mla_sample_kernel, n_pages=n_pages),
        grid_spec=grid_spec,
        out_shape=jax.ShapeDtypeStruct((b, MLA_HEADS, MLA_KV_RANK), BF16),
        compiler_params=_cparams("arbitrary"),
    )(page_table, *([cache_t] * n_pages), q_cat, row_new)


def _moba_sample_kernel(pt_ref, *refs, n_pages, n_blocks):
    pages = refs[:n_pages]
    q_ref, qcol_ref, kvnew_ref, bias_ref, bias0_ref, o_ref, m_ref, l_ref, g_ref, acc_ref = refs[n_pages:]
    c = pl.program_id(1)
    width = MOBA_HEADS * HEAD_DIM
    per_blk = MOBA_BLOCK // PAGE_SIZE
    blk_per_step = n_pages // per_blk
    keys = n_pages * PAGE_SIZE
    q = q_ref[...]
    qb = q.astype(BF16)
    qcol = qcol_ref[...]
    lane = lax.broadcasted_iota(jnp.int32, (8, LANES), 1)
    row = lax.broadcasted_iota(jnp.int32, (8, LANES), 0)

    @pl.when(c == 0)
    def _():
        m_ref[...] = jnp.full_like(m_ref, NEG_INF)
        l_ref[...] = jnp.zeros_like(l_ref)
        g_ref[...] = jnp.full_like(g_ref, NEG_INF)

    s = jnp.concatenate([_dot(qb, pages[u][0:width, :].astype(BF16)) for u in range(n_pages)], axis=-1)
    s = s * (1.0 / math.sqrt(HEAD_DIM)) + bias_ref[...]
    key_blk = lax.broadcasted_iota(jnp.int32, (8, keys), 1) // MOBA_BLOCK
    m_key = jnp.zeros((8, keys), F32)
    m_blk = []
    for j in range(blk_per_step):
        m_j = jnp.max(jnp.where(key_blk == j, s, NEG_INF), axis=-1, keepdims=True)
        m_blk.append(m_j)
        m_key = jnp.where(key_blk == j, m_j, m_key)
    p = jnp.exp(s - m_key)
    m_all, l_all, g_all = m_ref[...], l_ref[...], g_ref[...]
    for j in range(blk_per_step):
        n = c * blk_per_step + j
        l_j = jnp.sum(jnp.where(key_blk == j, p, 0.0), axis=-1, keepdims=True)
        part = pages[per_blk * j][0:width, :] * qcol
        for u in range(1, per_blk):
            part = part + pages[per_blk * j + u][0:width, :] * qcol
        part = jnp.sum(part.reshape(MOBA_HEADS, HEAD_DIM // 8, 8, PAGE_SIZE), axis=1)
        acc = jnp.zeros((8, width), F32)
        for u in range(per_blk):
            pu = p[:, (per_blk * j + u) * PAGE_SIZE:(per_blk * j + u + 1) * PAGE_SIZE].astype(BF16)
            acc = acc + _dot_nt(pu, pages[per_blk * j + u][width:2 * width, :].astype(BF16))
        acc_ref[n] = acc
        m_all = jnp.where(lane == n, m_blk[j], m_all)
        l_all = jnp.where(lane == n, l_j, l_all)
        for h in range(MOBA_HEADS):
            g_h = jnp.sum(jnp.sum(part[h], axis=-1, keepdims=True), axis=0, keepdims=True) * (1.0 / MOBA_BLOCK)
            g_all = jnp.where((lane == n) & (row == h), g_h, g_all)
    m_ref[...] = m_all
    l_ref[...] = l_all
    g_ref[...] = g_all

    @pl.when(c == pl.num_programs(1) - 1)
    def _():
        kv_new = kvnew_ref[...]
        s_self = (jnp.sum(q * kv_new[:, :width], axis=-1, keepdims=True) * (1.0 / math.sqrt(HEAD_DIM))
                  + bias0_ref[...][:, :1])
        past_blk = lane < n_blocks
        score = jnp.where(past_blk, g_all, NEG_INF)
        sel = (_rank_before(score, n_blocks) < MOBA_TOPK) & past_blk
        m_tot = jnp.maximum(jnp.max(jnp.where(sel, m_all, NEG_INF), axis=-1, keepdims=True), s_self)
        w = jnp.where(sel, jnp.exp(m_all - m_tot), 0.0)
        e_self = jnp.exp(s_self - m_tot)
        den = jnp.sum(w * l_all, axis=-1, keepdims=True) + e_self
        num = e_self * kv_new[:, width:]
        for n in range(n_blocks):
            num = num + w[:, n:n + 1] * acc_ref[n]
        o = num / den
        row_w = lax.broadcasted_iota(jnp.int32, (8, width), 0)
        lane_w = lax.broadcasted_iota(jnp.int32, (8, width), 1)
        o_ref[...] = jnp.sum(jnp.where(lane_w // HEAD_DIM == row_w, o, 0.0), axis=0, keepdims=True).astype(o_ref.dtype)


def _moba_sample(cache_t, layer, page_table, q_bd, q_col, kv_new, bias, bias0):
    b, n_pages = page_table.shape
    width = MOBA_HEADS * HEAD_DIM
    past = n_pages * PAGE_SIZE
    assert past % MOBA_BLOCK == 0 and past // MOBA_BLOCK <= LANES
    n_blocks = past // MOBA_BLOCK
    pages_per_step = min(16, n_pages)
    steps = n_pages // pages_per_step
    grid_spec = pltpu.PrefetchScalarGridSpec(
        num_scalar_prefetch=1, grid=(b, steps),
        in_specs=_page_specs(pages_per_step, layer, (2 * width, PAGE_SIZE), lambda c: c * pages_per_step)
        + [pl.BlockSpec((None, 8, width), lambda i, c, pt: (i, 0, 0)),
           pl.BlockSpec((None, width, PAGE_SIZE), lambda i, c, pt: (i, 0, 0)),
           pl.BlockSpec((None, 1, 2 * width), lambda i, c, pt: (i, 0, 0)),
           pl.BlockSpec((8, pages_per_step * PAGE_SIZE), lambda i, c, pt: (0, c)),
           pl.BlockSpec((8, LANES), lambda i, c, pt: (0, 0))],
        out_specs=pl.BlockSpec((None, 1, width), lambda i, c, pt: (i, 0, 0)),
        scratch_shapes=[pltpu.VMEM((8, LANES), F32), pltpu.VMEM((8, LANES), F32), pltpu.VMEM((8, LANES), F32),
                        pltpu.VMEM((n_blocks, 8, width), F32)])
    return pl.pallas_call(
        functools.partial(_moba_sample_kernel, n_pages=pages_per_step, n_blocks=n_blocks),
        grid_spec=grid_spec,
        out_shape=jax.ShapeDtypeStruct((b, 1, width), BF16),
        compiler_params=_cparams("arbitrary", "arbitrary"),
    )(page_table, *([cache_t] * pages_per_step), q_bd, q_col, kv_new, bias, bias0)


def _nsa_sample_kernel(pt_ref, *refs, n_pages, n_slc):
    pages = refs[:n_pages]
    (q_ref, g_ref, new_ref, win_ref, w16_ref, pos_ref, w1f_ref, b1_ref, w2_ref, b2_ref, cbias_ref, sbias_ref,
     wbias_ref, bias0_ref, ov_ref, ind_ref, perm_ref, o_ref, wout_ref, cmp_ref, slc_ref) = refs[n_pages:]
    past = n_pages * PAGE_SIZE
    nsub = past // NSA_CMP_STRIDE
    hid = NSA_CMP_HIDDEN
    scale = 1.0 / math.sqrt(NSA_DK)
    nwin = win_ref.shape[1]
    two = 2 * NSA_DK
    st = NSA_CMP_STRIDE
    sub_pair = 2 * PAGE_SIZE // st
    for i in range(0, n_pages, 2):
        pair = jnp.concatenate([pages[i][...], pages[i + 1][...]], axis=-1)
        by_phase = _dot_nt(perm_ref[...], pair[:two, :].astype(BF16))
        for j in range(st):
            cmp_ref[j, i * PAGE_SIZE // st:i * PAGE_SIZE // st + sub_pair, :] = (
                by_phase[j * sub_pair:(j + 1) * sub_pair, :].astype(BF16))
        slc_ref[:, i * PAGE_SIZE:(i + 2) * PAGE_SIZE] = pair[two:, :].astype(BF16)

    q8 = q_ref[...].astype(BF16)
    q8f = q_ref[...]
    new = new_ref[...]
    head_row = lax.broadcasted_iota(jnp.int32, (8, 1), 0) < NSA_HEADS
    bias0 = bias0_ref[...][:, :1]

    hs = jnp.zeros((nsub, 4 * hid), F32)
    for j in range(NSA_CMP_STRIDE):
        hs = hs + _dot(cmp_ref[j], w16_ref[j])
    hs_new = _dot(jnp.broadcast_to(new[:, :two], (8, two)).astype(BF16), w16_ref[0])[0:1, :]
    sub_i = lax.broadcasted_iota(jnp.int32, (nsub, hid), 0)
    outs = []
    for c in range(2):
        first = hs[:, 2 * c * hid:(2 * c + 1) * hid]
        second = pltpu.roll(hs[:, (2 * c + 1) * hid:(2 * c + 2) * hid], nsub - 1, 0)
        second = jnp.where(sub_i == nsub - 1, hs_new[:, (2 * c + 1) * hid:(2 * c + 2) * hid], second)
        h = first + second + _hpos(pos_ref, w1f_ref, c) + b1_ref[c]
        outs.append(_dot(jax.nn.silu(h).astype(BF16), w2_ref[c]) + b2_ref[c])
    kvc = jnp.concatenate(outs, axis=-1).astype(BF16)

    lc = _dot_nt(q8, kvc) * scale + cbias_ref[...]
    c_end = lax.broadcasted_iota(jnp.int32, (8, nsub), 1) * NSA_CMP_STRIDE + NSA_CMP_LEN - 1
    p_c = _masked_softmax(lc, c_end <= past)
    o_c = _dot(p_c.astype(BF16), kvc)
    p_sum = jnp.broadcast_to(jnp.sum(jnp.where(head_row, p_c, 0.0), axis=0, keepdims=True), (8, nsub))
    imp = _dot(p_sum, ov_ref[...], precision=lax.Precision.HIGHEST)

    jj = lax.broadcasted_iota(jnp.int32, imp.shape, 1)
    cur = past // NSA_SLC_BLOCK
    s_ok = jj <= cur
    forced = (jj == 0) | (jj == cur) | (jj == cur - 1)
    score = jnp.where(forced, jnp.inf, jnp.where(s_ok, imp, NEG_INF))
    sel = ((_rank_before(score, n_slc) < min(NSA_TOPN, n_slc)) & s_ok).astype(BF16)

    slc = slc_ref[...]
    picked = _dot(sel, ind_ref[...])
    s_all = jnp.where(picked > 0.5, _dot(q8, slc) * scale + sbias_ref[...], NEG_INF)
    s_self = jnp.sum(q8f * new[:, two:2 * two], axis=-1, keepdims=True) * scale + bias0
    m = jnp.maximum(jnp.max(s_all, axis=-1, keepdims=True), s_self)
    p = jnp.exp(s_all - m)
    e_self = jnp.exp(s_self - m)
    den = jnp.sum(p, axis=-1, keepdims=True) + e_self
    o_s = (_dot_nt(p.astype(BF16), slc) + e_self * new[:, two:2 * two]) / den

    win = win_ref[...]
    winb = win.astype(BF16)
    lw = _dot(q8, winb) * scale + wbias_ref[...]
    d_w = nwin - lax.broadcasted_iota(jnp.int32, (8, nwin), 1)
    lw = jnp.where(d_w < NSA_WINDOW, lw, NEG_INF)
    lw_self = jnp.sum(q8f * new[:, 2 * two:3 * two], axis=-1, keepdims=True) * scale + bias0
    m = jnp.maximum(jnp.max(lw, axis=-1, keepdims=True), lw_self)
    p = jnp.exp(lw - m)
    e_self = jnp.exp(lw_self - m)
    den = jnp.sum(p, axis=-1, keepdims=True) + e_self
    o_w = (_dot_nt(p.astype(BF16), winb) + e_self * new[:, 2 * two:3 * two]) / den

    g = jax.nn.sigmoid(g_ref[...])
    o_ref[...] = (g[:, 0:1] * o_c + g[:, 1:2] * o_s + g[:, 2:3] * o_w).astype(o_ref.dtype)

    rr = lax.broadcasted_iota(jnp.int32, (two, two), 0)
    cc = lax.broadcasted_iota(jnp.int32, (two, two), 1)
    new_col = jnp.sum(jnp.where(rr == cc, jnp.broadcast_to(new[:, 2 * two:3 * two], (two, two)), 0.0),
                      axis=-1, keepdims=True)
    col = lax.broadcasted_iota(jnp.int32, (two, nwin), 1)
    wout_ref[...] = jnp.where(col == nwin - 1, new_col, pltpu.roll(win, nwin - 1, 1))


def _nsa_sample(cache_t, layer, page_table, win_t, q8, g8, new, w16, posf, w1f, b1, w2, b2,
                cbias, sbias, wbias, bias0, overlap, n_slc):
    b, n_pages = page_table.shape
    past = n_pages * PAGE_SIZE
    rows = cache_t.shape[2]
    nwin = win_t.shape[-1]
    ind = _block_indicator(overlap.shape[1], n_slc, NSA_SLC_BLOCK, past)
    assert n_pages % 2 == 0
    st = NSA_CMP_STRIDE
    sub_pair = 2 * PAGE_SIZE // st
    r = np.arange(2 * PAGE_SIZE)
    perm = jnp.asarray((r[None, :] == (r[:, None] % sub_pair) * st + r[:, None] // sub_pair).astype(np.float32),
                       dtype=BF16)
    full = lambda a: pl.BlockSpec(a.shape, lambda i, pt: (0,) * a.ndim)
    grid_spec = pltpu.PrefetchScalarGridSpec(
        num_scalar_prefetch=1, grid=(b,),
        in_specs=_page_specs(n_pages, layer, (rows, PAGE_SIZE), lambda: 0)
        + [pl.BlockSpec((None, 8, LANES), lambda i, pt: (i, 0, 0)),
           pl.BlockSpec((None, 8, LANES), lambda i, pt: (i, 0, 0)),
           pl.BlockSpec((None, 1, new.shape[-1]), lambda i, pt: (i, 0, 0)),
           pl.BlockSpec((None, None, 2 * NSA_DK, nwin), lambda i, pt: (layer, i, 0, 0)),
           full(w16), full(posf), full(w1f), full(b1), full(w2), full(b2),
           full(cbias), full(sbias), full(wbias), full(bias0), full(overlap), full(ind), full(perm)],
        out_specs=[pl.BlockSpec((None, 8, LANES), lambda i, pt: (i, 0, 0)),
                   pl.BlockSpec((None, 2 * NSA_DK, nwin), lambda i, pt: (i, 0, 0))],
        scratch_shapes=[pltpu.VMEM((st, past // st, 2 * NSA_DK), BF16), pltpu.VMEM((2 * NSA_DK, past), BF16)])
    return pl.pallas_call(
        functools.partial(_nsa_sample_kernel, n_pages=n_pages, n_slc=n_slc),
        grid_spec=grid_spec,
        out_shape=[jax.ShapeDtypeStruct((b, 8, LANES), F32),
                   jax.ShapeDtypeStruct((b, 2 * NSA_DK, nwin), F32)],
        compiler_params=_cparams("arbitrary"),
    )(page_table, *([cache_t] * n_pages), q8, g8, new, win_t, w16, posf, w1f, b1, w2, b2,
      cbias, sbias, wbias, bias0, overlap, ind, perm)


def _rot_half_cols(w):
    half = MLA_ROPE // 2
    return jnp.concatenate([-w[..., half:], w[..., :half]], axis=-1)


def _pack_w_in(w):
    d = w.shape[0]
    sizes = (256, 256, 256, 256, 384, 12, 256, 128, 32, 3 * d)
    offs = np.concatenate([[0], np.cumsum(sizes)])
    mq, mk, mv, nq, nkv, ng, cq, ckv, kpe, mg = [w[:, offs[i]:offs[i + 1]] for i in range(10)]
    z = lambda n: jnp.zeros((d, n), w.dtype)
    out = jnp.concatenate([mq, mk, mv, nq, nkv, ng, z(C_CQ - C_NG - 12), cq, ckv, kpe, _rot_half_cols(kpe),
                           z(C_MG - C_KPER - MLA_ROPE), mg], axis=1)
    assert out.shape[1] == N_PAD
    return out.astype(BF16)


def _pack_w16(w1):
    st = NSA_CMP_STRIDE
    blocks = []
    for c in range(2):
        wc = jnp.concatenate([w1[c, :st], w1[c, st:]], axis=-1)
        zc = jnp.zeros_like(wc)
        blocks.append(jnp.concatenate([wc, zc] if c == 0 else [zc, wc], axis=-1))
    return jnp.concatenate(blocks, axis=1).astype(BF16)


def _rope_tables(pos, n_rep):
    half = MLA_ROPE // 2
    freq = ROPE_THETA ** (-jnp.arange(half, dtype=F32) / half)
    ang = pos.astype(F32)[:, None] * freq[None, :]
    cos = jnp.tile(jnp.cos(ang), (1, 2 * n_rep))
    sin = jnp.tile(jnp.sin(ang), (1, 2 * n_rep))
    return cos, sin


def _overlap_matrix(nc, n_slc, n_cols):
    ci = np.arange(nc)[:, None] * NSA_CMP_STRIDE
    sj = np.arange(n_cols)[None, :] * NSA_SLC_BLOCK
    ov = (ci < sj + NSA_SLC_BLOCK) & (ci + NSA_CMP_LEN > sj) & (np.arange(n_cols)[None, :] < n_slc)
    return jnp.asarray(ov.astype(np.float32))


def _slot_minor_views(cache_moba_kv, cache_nsa_kv, cache_mla, state_nsa_win):
    d, pool = cache_moba_kv.shape[:2]
    moba_t = jnp.transpose(cache_moba_kv, (0, 1, 3, 4, 5, 2)).reshape(d, pool, -1, PAGE_SIZE)
    nsa_t = jnp.transpose(cache_nsa_kv, (0, 1, 3, 4, 2)).reshape(d, pool, -1, PAGE_SIZE)
    mla_t = jnp.transpose(cache_mla, (0, 1, 3, 2))
    nwin = state_nsa_win.shape[2]
    win_t = jnp.transpose(state_nsa_win, (0, 1, 3, 4, 2)).reshape(d, state_nsa_win.shape[1], -1, nwin)
    return moba_t, nsa_t, mla_t, win_t


def kernel(x_prompt, x_sample, cache_moba_kv, cache_nsa_kv, cache_mla, state_nsa_win, page_table,
           p_prompt, p_sample, t5_bias, norm_mix, w_in, mla_q_norm, mla_kv_norm, w_q_up, w_kv_up,
           nsa_cmp_w1, nsa_cmp_b1, nsa_cmp_w2, nsa_cmp_b2, nsa_cmp_pos, w_branch, w_out, norm_ffn,
           w_ffn_gate, w_ffn_up, w_ffn_down, norm_ple, w_ple_gate, w_ple_proj, norm_final):
    depth = w_in.shape[0]
    bsz, seq, d = x_prompt.shape
    dec_b, dec_s, _ = x_sample.shape
    n_pages = page_table.shape[1]
    past = n_pages * PAGE_SIZE
    nwin = state_nsa_win.shape[2]
    assert dec_s == 1 and nwin == NSA_WINDOW and d == C_MG // 2
    wp = min(NSA_WINDOW, seq)
    page_table = page_table.astype(jnp.int32)
    moba_t, nsa_t, mla_t, win_t = _slot_minor_views(cache_moba_kv, cache_nsa_kv, cache_mla, state_nsa_win)

    nt = seq // ATT_TILE
    n_delta = max(_n_distinct_deltas(ATT_TILE, nt), min(3, nt))
    tiles = _lookup(t5_bias, _toeplitz_buckets(ATT_TILE, n_delta), 1)
    bias_moba_tiles = tiles[:, :MOBA_HEADS]
    bias_nsa_tiles = tiles[:, MOBA_HEADS:]
    nc_p = seq // NSA_CMP_STRIDE
    c_end_p = np.arange(nc_p) * NSA_CMP_STRIDE + NSA_CMP_LEN - 1
    cbias_p = _lookup(t5_bias[:, MOBA_HEADS:], _t5_bucket_np(np.arange(seq)[:, None] - c_end_p[None, :]), 0)
    n_slc_p = -(-seq // NSA_SLC_BLOCK)
    overlap_p = _overlap_matrix(nc_p, n_slc_p, LANES)

    def head_rows(idx, lo):
        a = _lookup(t5_bias[:, lo:lo + 4], idx, 0)
        return jnp.concatenate([a, jnp.zeros_like(a)], axis=0)

    dist_bucket = _t5_bucket_np(past - np.arange(past))
    sbias_moba = head_rows(dist_bucket, 0)
    sbias_nsa = head_rows(dist_bucket, MOBA_HEADS)
    nsub_s = past // NSA_CMP_STRIDE
    c_end_s = np.arange(nsub_s) * NSA_CMP_STRIDE + NSA_CMP_LEN - 1
    cbias_s = head_rows(_t5_bucket_np(past - c_end_s), MOBA_HEADS)
    wbias_s = head_rows(_t5_bucket_np(nwin - np.arange(nwin)), MOBA_HEADS)
    zero_bucket = np.full((LANES,), _t5_bucket_np(np.array(0)), np.int32)
    bias0_moba = head_rows(zero_bucket, 0)
    bias0_nsa = head_rows(zero_bucket, MOBA_HEADS)
    n_slc_s = -(-(past + dec_s) // NSA_SLC_BLOCK)
    slc_cols = -(-n_slc_s // LANES) * LANES
    overlap_s = _overlap_matrix(nsub_s, n_slc_s, slc_cols)

    cos_p, sin_p = _rope_tables(jnp.arange(seq, dtype=jnp.int32), MLA_HEADS)
    cos_s, sin_s = _rope_tables(jnp.full((dec_b,), past, jnp.int32), MLA_HEADS)

    xp = x_prompt.reshape(bsz * seq, d)
    xs = x_sample.reshape(dec_b, d)
    outs = [[] for _ in range(8)]
    for l in range(depth):
        w_in_p = _pack_w_in(w_in[l])
        wq = w_q_up[l].reshape(MLA_Q_RANK, MLA_HEADS, MLA_NOPE + MLA_ROPE)
        wq_nope = wq[:, :, :MLA_NOPE].reshape(MLA_Q_RANK, -1).astype(BF16)
        wq_pe = wq[:, :, MLA_NOPE:]
        wq_per = _rot_half_cols(wq_pe).reshape(MLA_Q_RANK, -1).astype(BF16)
        wq_pe = wq_pe.reshape(MLA_Q_RANK, -1).astype(BF16)
        wk_abs = jnp.transpose(w_kv_up[l][:, :, :MLA_NOPE], (1, 2, 0)).astype(BF16)
        wuv = jnp.transpose(w_kv_up[l][:, :, MLA_NOPE:], (1, 0, 2)).astype(BF16)
        g_mix = norm_mix[l][None, :]
        qn = mla_q_norm[l][None, :]
        kn = mla_kv_norm[l][None, :]
        w16 = _pack_w16(nsa_cmp_w1[l])
        posf = nsa_cmp_pos[l].reshape(2, 1, NSA_CMP_LEN * NSA_DK)
        w1f = nsa_cmp_w1[l].reshape(2, NSA_CMP_LEN * NSA_DK, NSA_CMP_HIDDEN)
        b1 = nsa_cmp_b1[l][:, None, :]
        w2 = nsa_cmp_w2[l].astype(BF16)
        b2 = nsa_cmp_b2[l][:, None, :]
        wb = w_branch[l].astype(BF16)
        wo = w_out[l].astype(BF16)
        ffn_w = (norm_ffn[l][None, :], w_ffn_gate[l].astype(BF16), w_ffn_up[l].astype(BF16),
                 w_ffn_down[l].astype(BF16), norm_ple[l][None, :], w_ple_gate[l].astype(BF16),
                 w_ple_proj[l].astype(BF16))
        last = l == depth - 1

        z, mg = _inproj(xp, g_mix, w_in_p)
        z3 = z.reshape(bsz, seq, -1)
        q_cat, mla_row, k_cat = _mla_prep(z, bsz, seq, qn, kn, wq_nope, wq_pe, wq_per, wk_abs, cos_p, sin_p)
        o_m = _moba_prompt(z, bsz, seq, bias_moba_tiles)
        kvc = _nsa_compress_prompt(z, bsz, seq, w16, posf, w1f, b1, w2, b2)
        o_n = _nsa_prompt(z, bsz, seq, kvc, cbias_p, bias_nsa_tiles, overlap_p)
        o_l = _mla_attn_prompt(q_cat, k_cat)
        xp3 = _mixer_out(xp.reshape(bsz, seq, d), mg, o_m, o_n, o_l, wuv, wb, wo)
        xp = _ffn_ple(xp3.reshape(bsz * seq, d), *ffn_w, p_prompt[l].reshape(bsz * seq, -1),
                      norm_final[None, :], last)
        outs[0].append(z3[:, :, C_MK:C_MK + 512].reshape(bsz, seq, 2, MOBA_HEADS, HEAD_DIM))
        outs[2].append(z3[:, :, C_NKV:C_NKV + 4 * NSA_DK].reshape(bsz, seq, 4, NSA_DK))
        outs[4].append(mla_row)
        outs[6].append(z3[:, seq - wp:, C_NKV + 4 * NSA_DK:C_NKV + 6 * NSA_DK].reshape(bsz, wp, 2, NSA_DK))

        zs, mg_s = _inproj(xs, g_mix, w_in_p)
        q_cat_s, mla_row_s, _ = _mla_prep(zs, 1, dec_b, qn, kn, wq_nope, wq_pe, wq_per, wk_abs, cos_s, sin_s)
        mq = zs[:, C_MQ:C_MQ + 256]
        eye = jnp.eye(MOBA_HEADS, dtype=F32)[None, :, :, None]
        q_bd = (mq.reshape(dec_b, MOBA_HEADS, 1, HEAD_DIM) * eye).reshape(dec_b, MOBA_HEADS, MOBA_HEADS * HEAD_DIM)
        q_bd = jnp.concatenate([q_bd, jnp.zeros_like(q_bd)], axis=1)
        q_col = jnp.broadcast_to(mq[:, :, None], (dec_b, mq.shape[1], PAGE_SIZE))
        o_m = _moba_sample(moba_t, l, page_table, q_bd, q_col, zs[:, None, C_MK:C_MK + 512],
                           sbias_moba, bias0_moba)
        nq = zs[:, C_NQ:C_NQ + 256].reshape(dec_b, NSA_HEADS, NSA_DK)
        q8 = jnp.pad(nq, ((0, 0), (0, 8 - NSA_HEADS), (0, LANES - NSA_DK)))
        g8 = jnp.pad(zs[:, C_NG:C_NG + 3 * NSA_HEADS].reshape(dec_b, NSA_HEADS, 3),
                     ((0, 0), (0, 8 - NSA_HEADS), (0, LANES - 3)))
        o_n8, win_new_t = _nsa_sample(nsa_t, l, page_table, win_t, q8, g8,
                                      zs[:, None, C_NKV:C_NKV + 6 * NSA_DK], w16, posf, w1f, b1, w2, b2,
                                      cbias_s, sbias_nsa, wbias_s, bias0_nsa, overlap_s, n_slc_s)
        o_n = o_n8[:, :NSA_HEADS, NSA_DK:].reshape(dec_b, NSA_HEADS * NSA_DK).astype(BF16)
        o_l = _mla_sample(mla_t, l, page_table, jnp.transpose(q_cat_s[0], (1, 0, 2)),
                          mla_row_s.reshape(dec_b, 1, -1))
        xs3 = _mixer_out(xs[None], mg_s, o_m.reshape(1, dec_b, -1), o_n[None],
                         jnp.transpose(o_l, (1, 0, 2))[None], wuv, wb, wo)
        xs = _ffn_ple(xs3[0], *ffn_w, p_sample[l].reshape(dec_b, -1), norm_final[None, :], last)
        outs[1].append(zs[:, C_MK:C_MK + 512].reshape(dec_b, 1, 2, MOBA_HEADS, HEAD_DIM))
        outs[3].append(zs[:, C_NKV:C_NKV + 4 * NSA_DK].reshape(dec_b, 1, 4, NSA_DK))
        outs[5].append(mla_row_s.reshape(dec_b, 1, -1))
        outs[7].append(jnp.transpose(win_new_t.reshape(dec_b, 2, NSA_DK, nwin), (0, 3, 1, 2)))

    return (xp.reshape(bsz, seq, d), xs.reshape(dec_b, dec_s, d)) + tuple(jnp.stack(o) for o in outs)
```
